```python
import jax, jax.numpy as jnp
from jax import lax
import numpy as np

D_MODEL = 1024
BATCH = 2
SEQ = 8192
DEPTH = 1

GRID_W = 64
HEAD_DIM = 64
ATTN_WIDTH = D_MODEL // 2
N_HEADS = ATTN_WIDTH // HEAD_DIM
N_KV_HEADS = max(1, N_HEADS // 4)
KV_WIDTH = N_KV_HEADS * HEAD_DIM
CONV_CH = D_MODEL - ATTN_WIDTH
CONV_GROUPS = 8
CONV_GROUP_DIM = CONV_CH // CONV_GROUPS
MIX_WIDTH = ATTN_WIDTH + CONV_CH
IN_COLS = ATTN_WIDTH + 2 * KV_WIDTH + 2 * CONV_CH
CONV_WIDTH = 31
FFN_CONV_WIDTH = 3
D_FF = (11 * D_MODEL // 4 + 127) // 128 * 128
ROPE_THETA = 10000.0
ROPE_FREQS = HEAD_DIM // 4
Q_BLOCK = 128
EPS = 1e-6
ALPHA = (2.0 * DEPTH) ** 0.25
BETA = (8.0 * DEPTH) ** -0.25

kernel_name = "hybrid_attn_conformer_convffn_deepnorm_adaln"


def layer_norm(x, g, b):
    xf = x.astype(jnp.float32)
    mu = jnp.mean(xf, axis=-1, keepdims=True)
    var = jnp.mean(jnp.square(xf - mu), axis=-1, keepdims=True)
    y = (xf - mu) * lax.rsqrt(var + EPS)
    return (y * g.astype(jnp.float32) + b.astype(jnp.float32)).astype(x.dtype)


def adaln_modulate(x, shift, scale):
    xf = x.astype(jnp.float32)
    mu = jnp.mean(xf, axis=-1, keepdims=True)
    var = jnp.mean(jnp.square(xf - mu), axis=-1, keepdims=True)
    y = ((xf - mu) * lax.rsqrt(var + EPS)).astype(x.dtype)
    return y * (1.0 + scale) + shift


def rms_norm(x, g):
    xf = x.astype(jnp.float32)
    y = xf * lax.rsqrt(jnp.mean(jnp.square(xf), axis=-1, keepdims=True) + EPS)
    return (y * g.astype(jnp.float32)).astype(x.dtype)


def depthwise_conv(x, w, b):
    width, ch = w.shape
    pad = (width - 1) // 2
    y = lax.conv_general_dilated(
        x, w.astype(x.dtype)[:, None, :], window_strides=(1,), padding=[(pad, pad)],
        dimension_numbers=('NWC', 'WIO', 'NWC'), feature_group_count=ch)
    return y + b.astype(x.dtype)


def axial_rope_tables(seq_len, dtype):
    n_rows = seq_len // GRID_W
    rows = jnp.repeat(jnp.arange(n_rows, dtype=jnp.float32), GRID_W)
    cols = jnp.tile(jnp.arange(GRID_W, dtype=jnp.float32), n_rows)
    inv_freq = ROPE_THETA ** (-jnp.arange(ROPE_FREQS, dtype=jnp.float32) / ROPE_FREQS)
    ang_r = rows[:, None, None] * inv_freq
    ang_c = cols[:, None, None] * inv_freq
    return (jnp.cos(ang_r).astype(dtype), jnp.sin(ang_r).astype(dtype),
            jnp.cos(ang_c).astype(dtype), jnp.sin(ang_c).astype(dtype))


def rotate(x, cos, sin):
    x1, x2 = jnp.split(x, 2, axis=-1)
    return jnp.concatenate([x1 * cos - x2 * sin, x2 * cos + x1 * sin], axis=-1)


def apply_axial_rope(x, tables):
    cos_r, sin_r, cos_c, sin_c = tables
    x_row, x_col = jnp.split(x, 2, axis=-1)
    return jnp.concatenate([rotate(x_row, cos_r, sin_r), rotate(x_col, cos_c, sin_c)], axis=-1)


def gqa_attention(q, k, v):
    b, s = q.shape[0], q.shape[1]
    groups = N_HEADS // N_KV_HEADS
    scale = HEAD_DIM ** -0.5
    qb = q.reshape(b, s // Q_BLOCK, Q_BLOCK, N_KV_HEADS, groups, HEAD_DIM)
    qb = jnp.moveaxis(qb, 1, 0)

    def one_block(q_blk):
        scores = jnp.einsum('bqhgd,bkhd->bhgqk', q_blk, k,
                            preferred_element_type=jnp.float32) * scale
        probs = jax.nn.softmax(scores, axis=-1).astype(v.dtype)
        return jnp.einsum('bhgqk,bkhd->bqhgd', probs, v)

    out = lax.map(one_block, qb)
    return jnp.moveaxis(out, 0, 1).reshape(b, s, N_HEADS, HEAD_DIM)


def setup_inputs(seed: int = 0) -> dict:
    key = jax.random.key(seed)
    ks = jax.random.split(key, 24)
    f32 = jnp.float32

    def nrm(k, shape, scale):
        return jax.random.normal(k, shape, f32) * scale

    def gain(k, shape):
        return 1.0 + 0.02 * jax.random.normal(k, shape, f32)

    x = jax.random.normal(ks[0], (BATCH, SEQ, D_MODEL), f32)
    c = jax.random.normal(ks[1], (BATCH, D_MODEL), f32)
    w_ada = nrm(ks[2], (DEPTH, D_MODEL, 6 * D_MODEL), 0.5 * D_MODEL ** -0.5)
    b_ada = nrm(ks[3], (DEPTH, 6 * D_MODEL), 0.01)
    col_scale = jnp.concatenate([jnp.ones((ATTN_WIDTH + KV_WIDTH,), f32),
                                 jnp.full((KV_WIDTH,), BETA, f32),
                                 jnp.ones((2 * CONV_CH,), f32)])
    w_in = nrm(ks[4], (DEPTH, D_MODEL, IN_COLS), D_MODEL ** -0.5) * col_scale
    q_norm_g = gain(ks[5], (DEPTH, HEAD_DIM))
    k_norm_g = gain(ks[6], (DEPTH, HEAD_DIM))
    conv_dw_w = nrm(ks[7], (DEPTH, CONV_WIDTH, CONV_CH), CONV_WIDTH ** -0.5)
    conv_dw_b = nrm(ks[8], (DEPTH, CONV_CH), 0.02)
    conv_ln_g = gain(ks[9], (DEPTH, CONV_CH))
    conv_ln_b = nrm(ks[10], (DEPTH, CONV_CH), 0.02)
    w_conv_pw2 = nrm(ks[11], (DEPTH, CONV_CH, CONV_CH), BETA * CONV_CH ** -0.5)
    attn_out_g = gain(ks[12], (DEPTH, N_HEADS, HEAD_DIM))
    conv_out_g = gain(ks[13], (DEPTH, CONV_GROUPS, CONV_GROUP_DIM))
    w_o = nrm(ks[14], (DEPTH, MIX_WIDTH, D_MODEL), BETA * MIX_WIDTH ** -0.5)
    ln1_g = gain(ks[15], (DEPTH, D_MODEL))
    ln1_b = nrm(ks[16], (DEPTH, D_MODEL), 0.02)
    w_up = nrm(ks[17], (DEPTH, D_MODEL, 2 * D_FF), D_MODEL ** -0.5)
    ffn_dw_w = nrm(ks[18], (DEPTH, FFN_CONV_WIDTH, 2 * D_FF), FFN_CONV_WIDTH ** -0.5)
    ffn_dw_b = nrm(ks[19], (DEPTH, 2 * D_FF), 0.02)
    w_down = nrm(ks[20], (DEPTH, D_FF, D_MODEL), BETA * D_FF ** -0.5)
    ln2_g = gain(ks[21], (DEPTH, D_MODEL))
    ln2_b = nrm(ks[22], (DEPTH, D_MODEL), 0.02)
    return {"x": x, "c": c, "w_ada": w_ada, "b_ada": b_ada, "w_in": w_in,
            "q_norm_g": q_norm_g, "k_norm_g": k_norm_g,
            "conv_dw_w": conv_dw_w, "conv_dw_b": conv_dw_b,
            "conv_ln_g": conv_ln_g, "conv_ln_b": conv_ln_b, "w_conv_pw2": w_conv_pw2,
            "attn_out_g": attn_out_g, "conv_out_g": conv_out_g, "w_o": w_o,
            "ln1_g": ln1_g, "ln1_b": ln1_b, "w_up": w_up,
            "ffn_dw_w": ffn_dw_w, "ffn_dw_b": ffn_dw_b, "w_down": w_down,
            "ln2_g": ln2_g, "ln2_b": ln2_b}


def reference(x, c, w_ada, b_ada, w_in, q_norm_g, k_norm_g, conv_dw_w, conv_dw_b,
              conv_ln_g, conv_ln_b, w_conv_pw2, attn_out_g, conv_out_g, w_o,
              ln1_g, ln1_b, w_up, ffn_dw_w, ffn_dw_b, w_down, ln2_g, ln2_b):
    b, s, _ = x.shape
    rope = axial_rope_tables(s, x.dtype)
    split_cols = [ATTN_WIDTH, ATTN_WIDTH + KV_WIDTH, ATTN_WIDTH + 2 * KV_WIDTH]
    c_act = jax.nn.silu(c)

    for l in range(DEPTH):
        mod = (c_act @ w_ada[l] + b_ada[l])[:, None, :]
        shift1, scale1, gate1, shift2, scale2, gate2 = jnp.split(mod, 6, axis=-1)

        u = adaln_modulate(x, shift1, scale1)
        proj = u @ w_in[l]
        q, k, v, glu = jnp.split(proj, split_cols, axis=-1)

        q = rms_norm(q.reshape(b, s, N_HEADS, HEAD_DIM), q_norm_g[l])
        k = rms_norm(k.reshape(b, s, N_KV_HEADS, HEAD_DIM), k_norm_g[l])
        v = v.reshape(b, s, N_KV_HEADS, HEAD_DIM)
        q = apply_axial_rope(q, rope)
        k = apply_axial_rope(k, rope)
        attn = gqa_attention(q, k, v)
        attn = rms_norm(attn, attn_out_g[l]).reshape(b, s, ATTN_WIDTH)

        a, g = jnp.split(glu, 2, axis=-1)
        h = a * jax.nn.sigmoid(g)
        h = depthwise_conv(h, conv_dw_w[l], conv_dw_b[l])
        h = layer_norm(h, conv_ln_g[l], conv_ln_b[l])
        h = jax.nn.silu(h) @ w_conv_pw2[l]
        h = rms_norm(h.reshape(b, s, CONV_GROUPS, CONV_GROUP_DIM), conv_out_g[l])
        h = h.reshape(b, s, CONV_CH)

        mixed = jnp.concatenate([attn, h], axis=-1) @ w_o[l]
        x = layer_norm(ALPHA * x + gate1 * mixed, ln1_g[l], ln1_b[l])

        u = adaln_modulate(x, shift2, scale2)
        hf = depthwise_conv(u @ w_up[l], ffn_dw_w[l], ffn_dw_b[l])
        val, gt = jnp.split(hf, 2, axis=-1)
        ffn = (jax.nn.gelu(gt, approximate=False) * val) @ w_down[l]
        x = layer_norm(ALPHA * x + gate2 * ffn, ln2_g[l], ln2_b[l])

    return x
```

```python
import functools

import jax
import jax.numpy as jnp
from jax import lax
from jax.experimental import pallas as pl
from jax.experimental.pallas import tpu as pltpu

GRID_W = 64
HEAD_DIM = 64
N_HEADS = 8
N_KV_HEADS = 2
GROUPS = N_HEADS // N_KV_HEADS
ATTN_WIDTH = N_HEADS * HEAD_DIM
KV_WIDTH = N_KV_HEADS * HEAD_DIM
CONV_GROUP_DIM = 64
CONV_WIDTH = 31
CONV_PAD = (CONV_WIDTH - 1) // 2
FFN_CONV_WIDTH = 3
ROPE_THETA = 10000.0
ROPE_FREQS = HEAD_DIM // 4
EPS = 1e-6

F32 = jnp.float32
BF16 = jnp.bfloat16

V7X_VMEM_LIMIT_BYTES = 56 * 1024 * 1024
ROW_BLOCK = 512
Q_BLOCK = 256
KV_CHUNK = ROW_BLOCK
HALO = 16
CONV_ROWS = 32
FF_CHUNK = 256


def _params(*sem):
    return pltpu.CompilerParams(dimension_semantics=sem, vmem_limit_bytes=V7X_VMEM_LIMIT_BYTES)


def _ln_noaffine(x):
    mu = jnp.mean(x, axis=-1, keepdims=True)
    xc = x - mu
    var = jnp.mean(xc * xc, axis=-1, keepdims=True)
    return xc * lax.rsqrt(var + EPS)


def _adaln_kernel(c_ref, w_ref, b_ref, o_ref):
    c = c_ref[...]
    c_act = c * jax.nn.sigmoid(c)
    o_ref[...] = jnp.dot(c_act, w_ref[...], preferred_element_type=F32,
                         precision=lax.Precision.HIGHEST) + b_ref[...]


def _adaln(c_pad, w, b):
    rows, d = c_pad.shape
    n = w.shape[1]
    return pl.pallas_call(
        _adaln_kernel,
        out_shape=jax.ShapeDtypeStruct((rows, n), F32),
        grid=(n // d,),
        in_specs=[pl.BlockSpec((rows, d), lambda j: (0, 0)),
                  pl.BlockSpec((d, d), lambda j: (0, j)),
                  pl.BlockSpec((1, d), lambda j: (0, j))],
        out_specs=pl.BlockSpec((rows, d), lambda j: (0, j)),
        compiler_params=_params("arbitrary"),
        name="adaln",
    )(c_pad, w, b)


def _rope_t(x, tab):
    f = ROPE_FREQS
    cr, sr, cc, sc = tab[0:f], tab[f:2 * f], tab[2 * f:3 * f], tab[3 * f:4 * f]
    x1r, x2r, x1c, x2c = x[0:f], x[f:2 * f], x[2 * f:3 * f], x[3 * f:4 * f]
    return jnp.concatenate([x1r * cr - x2r * sr, x2r * cr + x1r * sr,
                            x1c * cc - x2c * sc, x2c * cc + x1c * sc], axis=0)


def _norm_rope_t(xt, gain, tab):
    ms = jnp.mean(xt * xt, axis=0, keepdims=True)
    return _rope_t(xt * lax.rsqrt(ms + EPS) * gain, tab)


def _inproj_kernel(x_ref, mod_ref, w_ref, gains_ref, tab_ref, qt_ref, k_ref, vt_ref, h_ref):
    x = x_ref[...]
    shift, scale = mod_ref[0:1, :], mod_ref[1:2, :]
    u = _ln_noaffine(x) * (1.0 + scale) + shift
    proj = jnp.dot(u.astype(BF16), w_ref[...], preferred_element_type=F32)

    qkv_w = ATTN_WIDTH + 2 * KV_WIDTH
    qkv_t = proj[:, :qkv_w].T
    tab = tab_ref[...]
    gq, gk = gains_ref[0], gains_ref[1]
    for h in range(N_HEADS):
        r = h * HEAD_DIM
        qh = _norm_rope_t(qkv_t[r:r + HEAD_DIM], gq, tab) * (HEAD_DIM ** -0.5)
        qt_ref[r:r + HEAD_DIM, :] = qh.astype(BF16)
    k_rot = []
    for h in range(N_KV_HEADS):
        r = ATTN_WIDTH + h * HEAD_DIM
        k_rot.append(_norm_rope_t(qkv_t[r:r + HEAD_DIM], gk, tab))
    k_ref[...] = jnp.concatenate(k_rot, axis=0).T.astype(BF16)
    vt_ref[...] = qkv_t[ATTN_WIDTH + KV_WIDTH:qkv_w].astype(BF16)

    conv_ch = (proj.shape[1] - qkv_w) // 2
    a = proj[:, qkv_w:qkv_w + conv_ch]
    g = proj[:, qkv_w + conv_ch:]
    h_ref[...] = a * jax.nn.sigmoid(g)


def _inproj(x, mod, w_bf, gains, tab):
    b, s, d = x.shape
    t = ROW_BLOCK
    n_cols = w_bf.shape[1]
    conv_ch = (n_cols - ATTN_WIDTH - 2 * KV_WIDTH) // 2
    return pl.pallas_call(
        _inproj_kernel,
        out_shape=(jax.ShapeDtypeStruct((b, ATTN_WIDTH, s), BF16),
                   jax.ShapeDtypeStruct((b, s, KV_WIDTH), BF16),
                   jax.ShapeDtypeStruct((b, s // t, KV_WIDTH, t), BF16),
                   jax.ShapeDtypeStruct((b, s, conv_ch), F32)),
        grid=(b, s // t),
        in_specs=[pl.BlockSpec((None, t, d), lambda bi, i: (bi, i, 0)),
                  pl.BlockSpec((None, 6, d), lambda bi, i: (bi, 0, 0)),
                  pl.BlockSpec((d, n_cols), lambda bi, i: (0, 0)),
                  pl.BlockSpec((2, HEAD_DIM, t), lambda bi, i: (0, 0, 0)),
                  pl.BlockSpec((HEAD_DIM, t), lambda bi, i: (0, i))],
        out_specs=(pl.BlockSpec((None, ATTN_WIDTH, t), lambda bi, i: (bi, 0, i)),
                   pl.BlockSpec((None, t, KV_WIDTH), lambda bi, i: (bi, i, 0)),
                   pl.BlockSpec((None, None, KV_WIDTH, t), lambda bi, i: (bi, i, 0, 0)),
                   pl.BlockSpec((None, t, conv_ch), lambda bi, i: (bi, i, 0))),
        compiler_params=_params("parallel", "parallel"),
        name="inproj",
    )(x, mod, w_bf, gains, tab)


def _attn_kernel(qt_ref, k_ref, vt_ref, g_ref, o_ref, qp_ref, m_ref, l_ref, acc_ref):
    grp = pl.program_id(1)
    n_chunks = k_ref.shape[0]
    tq = qt_ref.shape[1]

    first = grp == 0
    for h in range(GROUPS):
        qh = qt_ref[h * HEAD_DIM:(h + 1) * HEAD_DIM, :]
        z = jnp.zeros_like(qh)
        qp_ref[h, 0:HEAD_DIM, :] = jnp.where(first, qh, z)
        qp_ref[h, HEAD_DIM:2 * HEAD_DIM, :] = jnp.where(first, z, qh)
    m_ref[...] = jnp.full(m_ref.shape, -jnp.inf, F32)
    l_ref[...] = jnp.zeros(l_ref.shape, F32)
    acc_ref[...] = jnp.zeros(acc_ref.shape, F32)

    def chunk(c, carry):
        kc = k_ref[c]
        vc = vt_ref[c]
        for h in range(GROUPS):
            s = jnp.dot(kc, qp_ref[h], preferred_element_type=F32)
            m_old = m_ref[h]
            m_new = jnp.maximum(m_old, jnp.max(s, axis=0, keepdims=True))
            p = jnp.exp(s - m_new)
            alpha = jnp.exp(m_old - m_new)
            l_ref[h] = alpha * l_ref[h] + jnp.sum(p, axis=0, keepdims=True)
            pv = jnp.dot(vc, p.astype(BF16), preferred_element_type=F32)
            acc_ref[h] = alpha * acc_ref[h] + pv
            m_ref[h] = m_new
        return carry

    lax.fori_loop(0, n_chunks, chunk, 0)

    outs = []
    for h in range(GROUPS):
        o = acc_ref[h] / l_ref[h]
        ms = jnp.mean(o * o, axis=0, keepdims=True)
        outs.append(o * lax.rsqrt(ms + EPS) * g_ref[h * HEAD_DIM:(h + 1) * HEAD_DIM, :])
    o_ref[...] = jnp.concatenate(outs, axis=0).T.astype(BF16)
    del tq


def _attention(qt, k4, vt4, gain_t):
    b, _, s = qt.shape
    n_chunks, tk = k4.shape[1], k4.shape[2]
    tq = Q_BLOCK
    gw = GROUPS * HEAD_DIM
    return pl.pallas_call(
        _attn_kernel,
        out_shape=jax.ShapeDtypeStruct((b, s, ATTN_WIDTH), BF16),
        grid=(b, N_KV_HEADS, s // tq),
        in_specs=[pl.BlockSpec((None, gw, tq), lambda bi, g, qi: (bi, g, qi)),
                  pl.BlockSpec((None, n_chunks, tk, KV_WIDTH), lambda bi, g, qi: (bi, 0, 0, 0)),
                  pl.BlockSpec((None, n_chunks, HEAD_DIM, tk), lambda bi, g, qi: (bi, 0, g, 0)),
                  pl.BlockSpec((None, gw, tq), lambda bi, g, qi: (g, 0, 0))],
        out_specs=pl.BlockSpec((None, tq, gw), lambda bi, g, qi: (bi, qi, g)),
        scratch_shapes=[pltpu.VMEM((GROUPS, 2 * HEAD_DIM, tq), BF16),
                        pltpu.VMEM((GROUPS, 1, tq), F32),
                        pltpu.VMEM((GROUPS, 1, tq), F32),
                        pltpu.VMEM((GROUPS, HEAD_DIM, tq), F32)],
        compiler_params=_params("parallel", "parallel", "parallel"),
        name="attn",
    )(qt, k4, vt4, gain_t)


def _group_mean(sq, gmat):
    hi = sq.astype(BF16)
    lo = (sq - hi.astype(F32)).astype(BF16)
    return (jnp.dot(hi, gmat, preferred_element_type=F32) +
            jnp.dot(lo, gmat, preferred_element_type=F32))


def _convmod_kernel(prev_ref, cur_ref, next_ref, dw_ref, vec_ref, pw_ref, gmat_ref, o_ref, ext_ref, z_ref):
    i = pl.program_id(1)
    last = pl.num_programs(1) - 1
    t = cur_ref.shape[0]
    ext_ref[0:HALO, :] = jnp.where(i > 0, prev_ref[...], 0.0)
    ext_ref[HALO:HALO + t, :] = cur_ref[...]
    ext_ref[HALO + t:, :] = jnp.where(i < last, next_ref[...], 0.0)

    dw_b, ln_g, ln_b, out_g = vec_ref[0:1, :], vec_ref[1:2, :], vec_ref[2:3, :], vec_ref[3:4, :]
    base = HALO - CONV_PAD
    for r in range(0, t, CONV_ROWS):
        acc = jnp.broadcast_to(dw_b, (CONV_ROWS, dw_b.shape[1]))
        for j in range(CONV_WIDTH):
            acc = acc + dw_ref[j:j + 1, :] * ext_ref[base + r + j:base + r + j + CONV_ROWS, :]
        y = _ln_noaffine(acc) * ln_g + ln_b
        z_ref[r:r + CONV_ROWS, :] = (y * jax.nn.sigmoid(y)).astype(BF16)

    pw = jnp.dot(z_ref[...], pw_ref[...], preferred_element_type=F32)
    ms = _group_mean(pw * pw, gmat_ref[...])
    o_ref[...] = (pw * lax.rsqrt(ms + EPS) * out_g).astype(BF16)


def _convmod(h, dw_w, vecs, pw_bf, gmat):
    b, s, ch = h.shape
    t = ROW_BLOCK
    hb = t // HALO
    n_halo = s // HALO
    return pl.pallas_call(
        _convmod_kernel,
        out_shape=jax.ShapeDtypeStruct((b, s, ch), BF16),
        grid=(b, s // t),
        in_specs=[pl.BlockSpec((None, HALO, ch), lambda bi, i: (bi, jnp.maximum(i * hb - 1, 0), 0)),
                  pl.BlockSpec((None, t, ch), lambda bi, i: (bi, i, 0)),
                  pl.BlockSpec((None, HALO, ch), lambda bi, i: (bi, jnp.minimum((i + 1) * hb, n_halo - 1), 0)),
                  pl.BlockSpec(dw_w.shape, lambda bi, i: (0, 0)),
                  pl.BlockSpec(vecs.shape, lambda bi, i: (0, 0)),
                  pl.BlockSpec(pw_bf.shape, lambda bi, i: (0, 0)),
                  pl.BlockSpec(gmat.shape, lambda bi, i: (0, 0))],
        out_specs=pl.BlockSpec((None, t, ch), lambda bi, i: (bi, i, 0)),
        scratch_shapes=[pltpu.VMEM((t + 2 * HALO, ch), F32),
                        pltpu.VMEM((t, ch), BF16)],
        compiler_params=_params("parallel", "parallel"),
        name="convmod",
    )(h, h, h, dw_w, vecs, pw_bf, gmat)


def _outproj_kernel(alpha, x_ref, a_ref, c_ref, mod_ref, wa_ref, wc_ref, ln_ref, x1_ref, u2_ref):
    mixed = (jnp.dot(a_ref[...], wa_ref[...], preferred_element_type=F32) +
             jnp.dot(c_ref[...], wc_ref[...], preferred_element_type=F32))
    gate1, shift2, scale2 = mod_ref[2:3, :], mod_ref[3:4, :], mod_ref[4:5, :]
    y = alpha * x_ref[...] + gate1 * mixed
    x1 = _ln_noaffine(y) * ln_ref[0:1, :] + ln_ref[1:2, :]
    x1_ref[...] = x1
    u2_ref[...] = (_ln_noaffine(x1) * (1.0 + scale2) + shift2).astype(BF16)


def _outproj(alpha, x, attn, hc, mod, wo_a, wo_c, ln1):
    b, s, d = x.shape
    t = ROW_BLOCK
    aw, cw = attn.shape[2], hc.shape[2]
    return pl.pallas_call(
        functools.partial(_outproj_kernel, alpha),
        out_shape=(jax.ShapeDtypeStruct((b, s, d), F32),
                   jax.ShapeDtypeStruct((b, s, d), BF16)),
        grid=(b, s // t),
        in_specs=[pl.BlockSpec((None, t, d), lambda bi, i: (bi, i, 0)),
                  pl.BlockSpec((None, t, aw), lambda bi, i: (bi, i, 0)),
                  pl.BlockSpec((None, t, cw), lambda bi, i: (bi, i, 0)),
                  pl.BlockSpec((None, 6, d), lambda bi, i: (bi, 0, 0)),
                  pl.BlockSpec((aw, d), lambda bi, i: (0, 0)),
                  pl.BlockSpec((cw, d), lambda bi, i: (0, 0)),
                  pl.BlockSpec((2, d), lambda bi, i: (0, 0))],
        out_specs=(pl.BlockSpec((None, t, d), lambda bi, i: (bi, i, 0)),
                   pl.BlockSpec((None, t, d), lambda bi, i: (bi, i, 0))),
        compiler_params=_params("parallel", "parallel"),
        name="outproj",
    )(x, attn, hc, mod, wo_a, wo_c, ln1)


def _ffn_kernel(alpha, prev_ref, cur_ref, next_ref, x1_ref, mod_ref, wup_ref, dw_ref, wdn_ref, ln_ref,
                o_ref, ext_ref, hv_ref, hg_ref):
    i = pl.program_id(1)
    last = pl.num_programs(1) - 1
    t = cur_ref.shape[0]
    d_ff = wdn_ref.shape[0]
    ext_ref[0:HALO, :] = jnp.where(i > 0, prev_ref[...], jnp.zeros_like(prev_ref[...]))
    ext_ref[HALO:HALO + t, :] = cur_ref[...]
    ext_ref[HALO + t:, :] = jnp.where(i < last, next_ref[...], jnp.zeros_like(next_ref[...]))
    ext = ext_ref[...]

    def conv3(h_ref, col):
        w = dw_ref[:, col:col + FF_CHUNK]
        return (w[0:1] * h_ref[HALO - 1:HALO - 1 + t, :] + w[1:2] * h_ref[HALO:HALO + t, :] +
                w[2:3] * h_ref[HALO + 1:HALO + 1 + t, :] + w[3:4])

    acc = jnp.zeros((t, o_ref.shape[1]), F32)
    for c in range(0, d_ff, FF_CHUNK):
        hv_ref[...] = jnp.dot(ext, wup_ref[:, c:c + FF_CHUNK], preferred_element_type=F32)
        hg_ref[...] = jnp.dot(ext, wup_ref[:, d_ff + c:d_ff + c + FF_CHUNK], preferred_element_type=F32)
        val = conv3(hv_ref, c)
        gt = conv3(hg_ref, d_ff + c)
        act = 0.5 * gt * (1.0 + lax.erf(gt * (2.0 ** -0.5))) * val
        acc = acc + jnp.dot(act.astype(BF16), wdn_ref[c:c + FF_CHUNK, :], preferred_element_type=F32)

    gate2 = mod_ref[5:6, :]
    y = alpha * x1_ref[...] + gate2 * acc
    o_ref[...] = _ln_noaffine(y) * ln_ref[0:1, :] + ln_ref[1:2, :]


def _ffn(alpha, u2, x1, mod, wup_bf, dw4, wdn_bf, ln2):
    b, s, d = x1.shape
    t = ROW_BLOCK
    hb = t // HALO
    n_halo = s // HALO
    resident = dict(pipeline_mode=pl.Buffered(1))
    return pl.pallas_call(
        functools.partial(_ffn_kernel, alpha),
        out_shape=jax.ShapeDtypeStruct((b, s, d), F32),
        grid=(b, s // t),
        in_specs=[pl.BlockSpec((None, HALO, d), lambda bi, i: (bi, jnp.maximum(i * hb - 1, 0), 0)),
                  pl.BlockSpec((None, t, d), lambda bi, i: (bi, i, 0)),
                  pl.BlockSpec((None, HALO, d), lambda bi, i: (bi, jnp.minimum((i + 1) * hb, n_halo - 1), 0)),
                  pl.BlockSpec((None, t, d), lambda bi, i: (bi, i, 0)),
                  pl.BlockSpec((None, 6, d), lambda bi, i: (bi, 0, 0)),
                  pl.BlockSpec(wup_bf.shape, lambda bi, i: (0, 0), **resident),
                  pl.BlockSpec(dw4.shape, lambda bi, i: (0, 0)),
                  pl.BlockSpec(wdn_bf.shape, lambda bi, i: (0, 0), **resident),
                  pl.BlockSpec((2, d), lambda bi, i: (0, 0))],
        out_specs=pl.BlockSpec((None, t, d), lambda bi, i: (bi, i, 0)),
        scratch_shapes=[pltpu.VMEM((t + 2 * HALO, d), BF16),
                        pltpu.VMEM((t + 2 * HALO, FF_CHUNK), F32),
                        pltpu.VMEM((t + 2 * HALO, FF_CHUNK), F32)],
        compiler_params=_params("parallel", "parallel"),
        name="ffn",
    )(u2, u2, u2, x1, mod, wup_bf, dw4, wdn_bf, ln2)


def _rope_table_t(seq_len):
    pos = jnp.arange(seq_len, dtype=jnp.int32)
    rows = (pos // GRID_W).astype(F32)
    cols = (pos % GRID_W).astype(F32)
    inv_freq = ROPE_THETA ** (-jnp.arange(ROPE_FREQS, dtype=F32) / ROPE_FREQS)
    ang_r = inv_freq[:, None] * rows[None, :]
    ang_c = inv_freq[:, None] * cols[None, :]
    return jnp.concatenate([jnp.cos(ang_r), jnp.sin(ang_r), jnp.cos(ang_c), jnp.sin(ang_c)], axis=0)


def kernel(x, c, w_ada, b_ada, w_in, q_norm_g, k_norm_g, conv_dw_w, conv_dw_b, conv_ln_g, conv_ln_b,
           w_conv_pw2, attn_out_g, conv_out_g, w_o, ln1_g, ln1_b, w_up, ffn_dw_w, ffn_dw_b, w_down,
           ln2_g, ln2_b):
    b, s, d = x.shape
    depth = w_ada.shape[0]
    alpha = (2.0 * depth) ** 0.25
    conv_ch = w_conv_pw2.shape[1]
    assert s % ROW_BLOCK == 0 and s % Q_BLOCK == 0 and s % GRID_W == 0
    assert w_down.shape[1] % FF_CHUNK == 0 and b <= 8

    tab = _rope_table_t(s)
    c_pad = jnp.zeros((8, d), F32).at[:b].set(c)
    gidx = jnp.arange(conv_ch) // CONV_GROUP_DIM
    gmat = jnp.where(gidx[:, None] == gidx[None, :], 1.0 / CONV_GROUP_DIM, 0.0).astype(BF16)

    for l in range(depth):
        mod = _adaln(c_pad, w_ada[l], b_ada[l][None, :])[:b].reshape(b, 6, d)

        gains = jnp.stack([jnp.broadcast_to(q_norm_g[l][:, None], (HEAD_DIM, ROW_BLOCK)),
                           jnp.broadcast_to(k_norm_g[l][:, None], (HEAD_DIM, ROW_BLOCK))])
        qt, k, vt4, h = _inproj(x, mod, w_in[l].astype(BF16), gains, tab)

        k4 = k.reshape(b, s // KV_CHUNK, KV_CHUNK, KV_WIDTH)
        gain_t = jnp.broadcast_to(attn_out_g[l].reshape(N_KV_HEADS, GROUPS * HEAD_DIM, 1),
                                  (N_KV_HEADS, GROUPS * HEAD_DIM, Q_BLOCK))
        attn = _attention(qt, k4, vt4, gain_t)

        dw_w = jnp.zeros((CONV_WIDTH + 1, conv_ch), F32).at[:CONV_WIDTH].set(conv_dw_w[l])
        vecs = jnp.stack([conv_dw_b[l], conv_ln_g[l], conv_ln_b[l], conv_out_g[l].reshape(-1)])
        hc = _convmod(h, dw_w, vecs, w_conv_pw2[l].astype(BF16), gmat)

        wo_bf = w_o[l].astype(BF16)
        x1, u2 = _outproj(alpha, x, attn, hc, mod, wo_bf[:ATTN_WIDTH], wo_bf[ATTN_WIDTH:],
                          jnp.stack([ln1_g[l], ln1_b[l]]))

        dw4 = jnp.concatenate([ffn_dw_w[l], ffn_dw_b[l][None, :]], axis=0)
        x = _ffn(alpha, u2, x1, mod, w_up[l].astype(BF16), dw4, w_down[l].astype(BF16),
                 jnp.stack([ln2_g[l], ln2_b[l]]))
    return x
```

```python
import functools

import jax
import jax.numpy as jnp
from jax import lax
from jax.experimental import pallas as pl
from jax.experimental.pallas import tpu as pltpu

GRID_W = 64
HEAD_DIM = 64
N_HEADS = 8
N_KV_HEADS = 2
GROUPS = N_HEADS // N_KV_HEADS
ATTN_WIDTH = N_HEADS * HEAD_DIM
KV_WIDTH = N_KV_HEADS * HEAD_DIM
CONV_GROUP_DIM = 64
CONV_WIDTH = 31
CONV_PAD = (CONV_WIDTH - 1) // 2
FFN_CONV_WIDTH = 3
ROPE_THETA = 10000.0
ROPE_FREQS = HEAD_DIM // 4
EPS = 1e-6
Q_SCALE = HEAD_DIM ** -0.5 * 1.4426950408889634
SUM_ROWS = 16
V_ROWS = HEAD_DIM + SUM_ROWS

F32 = jnp.float32
BF16 = jnp.bfloat16

V7X_VMEM_LIMIT_BYTES = 56 * 1024 * 1024
ROW_BLOCK = 512
Q_BLOCK = 512
KV_CHUNK = ROW_BLOCK
HALO = 16
CONV_ROWS = 32
FF_CHUNK = 256
MAX_SLAB = 64
SCORES_AHEAD = 2


def _params(*sem):
    return pltpu.CompilerParams(dimension_semantics=sem, vmem_limit_bytes=V7X_VMEM_LIMIT_BYTES)


def _ln_noaffine(x):
    mu = jnp.mean(x, axis=-1, keepdims=True)
    xc = x - mu
    var = jnp.mean(xc * xc, axis=-1, keepdims=True)
    return xc * lax.rsqrt(var + EPS)


def _adaln_kernel(c_ref, w_ref, b_ref, o_ref):
    c = c_ref[...]
    c_act = c * jax.nn.sigmoid(c)
    o_ref[...] = jnp.dot(c_act, w_ref[...], preferred_element_type=F32,
                         precision=lax.Precision.HIGHEST) + b_ref[...]


def _adaln(c_pad, w, b):
    rows, d = c_pad.shape
    n = w.shape[1]
    return pl.pallas_call(
        _adaln_kernel,
        out_shape=jax.ShapeDtypeStruct((rows, n), F32),
        grid=(n // d,),
        in_specs=[pl.BlockSpec((rows, d), lambda j: (0, 0)),
                  pl.BlockSpec((d, d), lambda j: (0, j)),
                  pl.BlockSpec((1, d), lambda j: (0, j))],
        out_specs=pl.BlockSpec((rows, d), lambda j: (0, j)),
        compiler_params=_params("arbitrary"),
        name="adaln",
    )(c_pad, w, b)


def _rope_t(x, tab):
    f = ROPE_FREQS
    cr, sr, cc, sc = tab[0:f], tab[f:2 * f], tab[2 * f:3 * f], tab[3 * f:4 * f]
    x1r, x2r, x1c, x2c = x[0:f], x[f:2 * f], x[2 * f:3 * f], x[3 * f:4 * f]
    return jnp.concatenate([x1r * cr - x2r * sr, x2r * cr + x1r * sr,
                            x1c * cc - x2c * sc, x2c * cc + x1c * sc], axis=0)


def _norm_rope_t(xt, gain, tab):
    ms = jnp.mean(xt * xt, axis=0, keepdims=True)
    return _rope_t(xt * lax.rsqrt(ms + EPS) * gain, tab)


def _inproj_kernel(x_ref, mod_ref, w_ref, gains_ref, tab_ref, qt_ref, k_ref, vt_ref, h_ref):
    x = x_ref[...]
    shift, scale = mod_ref[0:1, :], mod_ref[1:2, :]
    u = _ln_noaffine(x) * (1.0 + scale) + shift
    proj = jnp.dot(u.astype(BF16), w_ref[...], preferred_element_type=F32)

    qkv_w = ATTN_WIDTH + 2 * KV_WIDTH
    qkv_t = proj[:, :qkv_w].T
    tab = tab_ref[...]
    gq, gk = gains_ref[0], gains_ref[1]
    for h in range(N_HEADS):
        r = h * HEAD_DIM
        qh = _norm_rope_t(qkv_t[r:r + HEAD_DIM], gq, tab) * Q_SCALE
        qt_ref[r:r + HEAD_DIM, :] = qh.astype(BF16)
    k_rot = []
    for h in range(N_KV_HEADS):
        r = ATTN_WIDTH + h * HEAD_DIM
        k_rot.append(_norm_rope_t(qkv_t[r:r + HEAD_DIM], gk, tab))
    k_ref[...] = jnp.concatenate(k_rot, axis=0).T.astype(BF16)
    for h in range(N_KV_HEADS):
        r = ATTN_WIDTH + KV_WIDTH + h * HEAD_DIM
        vt_ref[h, 0:HEAD_DIM, :] = qkv_t[r:r + HEAD_DIM].astype(BF16)
        vt_ref[h, HEAD_DIM:, :] = jnp.ones((SUM_ROWS, vt_ref.shape[2]), BF16)

    conv_ch = (proj.shape[1] - qkv_w) // 2
    a = proj[:, qkv_w:qkv_w + conv_ch]
    g = proj[:, qkv_w + conv_ch:]
    h_ref[...] = a * jax.nn.sigmoid(g)


def _inproj(x, mod, w_bf, gains, tab):
    b, s, d = x.shape
    t = ROW_BLOCK
    n_cols = w_bf.shape[1]
    conv_ch = (n_cols - ATTN_WIDTH - 2 * KV_WIDTH) // 2
    return pl.pallas_call(
        _inproj_kernel,
        out_shape=(jax.ShapeDtypeStruct((b, ATTN_WIDTH, s), BF16),
                   jax.ShapeDtypeStruct((b, s, KV_WIDTH), BF16),
                   jax.ShapeDtypeStruct((b, s // t, N_KV_HEADS, V_ROWS, t), BF16),
                   jax.ShapeDtypeStruct((b, s, conv_ch), F32)),
        grid=(b, s // t),
        in_specs=[pl.BlockSpec((None, t, d), lambda bi, i: (bi, i, 0)),
                  pl.BlockSpec((None, 6, d), lambda bi, i: (bi, 0, 0)),
                  pl.BlockSpec((d, n_cols), lambda bi, i: (0, 0)),
                  pl.BlockSpec((2, HEAD_DIM, t), lambda bi, i: (0, 0, 0)),
                  pl.BlockSpec((HEAD_DIM, t), lambda bi, i: (0, i))],
        out_specs=(pl.BlockSpec((None, ATTN_WIDTH, t), lambda bi, i: (bi, 0, i)),
                   pl.BlockSpec((None, t, KV_WIDTH), lambda bi, i: (bi, i, 0)),
                   pl.BlockSpec((None, None, N_KV_HEADS, V_ROWS, t), lambda bi, i: (bi, i, 0, 0, 0)),
                   pl.BlockSpec((None, t, conv_ch), lambda bi, i: (bi, i, 0))),
        compiler_params=_params("parallel", "parallel"),
        name="inproj",
    )(x, mod, w_bf, gains, tab)


def _attn_kernel(qt_ref, k_ref, vt_ref, g_ref, o_ref, qp_ref, s_ref, m_ref, acc_ref):
    grp = pl.program_id(1)
    n_chunks, tk = k_ref.shape[0], k_ref.shape[1]
    tq = qt_ref.shape[1]

    first = grp == 0
    for h in range(GROUPS):
        qh = qt_ref[h * HEAD_DIM:(h + 1) * HEAD_DIM, :]
        z = jnp.zeros_like(qh)
        qp_ref[h, 0:HEAD_DIM, :] = jnp.where(first, qh, z)
        qp_ref[h, HEAD_DIM:2 * HEAD_DIM, :] = jnp.where(first, z, qh)
    m_ref[...] = jnp.full(m_ref.shape, -jnp.inf, F32)
    acc_ref[...] = jnp.zeros(acc_ref.shape, F32)

    def scores(c, h):
        return jnp.dot(k_ref[c], qp_ref[h], preferred_element_type=F32)

    for h in range(SCORES_AHEAD):
        s_ref[h] = scores(0, h)

    def chunk(c, carry):
        vc = vt_ref[c]
        c_next = jnp.minimum(c + 1, n_chunks - 1)
        for h in range(GROUPS):
            ahead = h + SCORES_AHEAD
            if ahead < GROUPS:
                s_ref[ahead] = scores(c, ahead)
            else:
                s_ref[ahead - GROUPS] = scores(c_next, ahead - GROUPS)
            s = s_ref[h]
            m_old = m_ref[h]
            slab_max = jnp.max(s.reshape(tk // MAX_SLAB, MAX_SLAB, tq), axis=0)
            m_new = jnp.maximum(m_old, jnp.max(slab_max, axis=0, keepdims=True))
            p = jnp.exp2(s - m_new)
            alpha = jnp.exp2(m_old - m_new)
            pv = jnp.dot(vc, p.astype(BF16), preferred_element_type=F32)
            acc_ref[h] = alpha * acc_ref[h] + pv
            m_ref[h] = m_new
        return carry

    lax.fori_loop(0, n_chunks, chunk, 0)

    outs = []
    for h in range(GROUPS):
        o = acc_ref[h, 0:HEAD_DIM, :] / acc_ref[h, HEAD_DIM:HEAD_DIM + 1, :]
        ms = jnp.mean(o * o, axis=0, keepdims=True)
        outs.append(o * lax.rsqrt(ms + EPS) * g_ref[h * HEAD_DIM:(h + 1) * HEAD_DIM, :])
    o_ref[...] = jnp.concatenate(outs, axis=0).T.astype(BF16)
    del tq


def _attention(qt, k4, vt4, gain_t):
    b, _, s = qt.shape
    n_chunks, tk = k4.shape[1], k4.shape[2]
    tq = Q_BLOCK
    gw = GROUPS * HEAD_DIM
    return pl.pallas_call(
        _attn_kernel,
        out_shape=jax.ShapeDtypeStruct((b, s, ATTN_WIDTH), BF16),
        grid=(b, N_KV_HEADS, s // tq),
        in_specs=[pl.BlockSpec((None, gw, tq), lambda bi, g, qi: (bi, g, qi)),
                  pl.BlockSpec((None, n_chunks, tk, KV_WIDTH), lambda bi, g, qi: (bi, 0, 0, 0)),
                  pl.BlockSpec((None, n_chunks, None, V_ROWS, tk), lambda bi, g, qi: (bi, 0, g, 0, 0)),
                  pl.BlockSpec((None, gw, tq), lambda bi, g, qi: (g, 0, 0))],
        out_specs=pl.BlockSpec((None, tq, gw), lambda bi, g, qi: (bi, qi, g)),
        scratch_shapes=[pltpu.VMEM((GROUPS, 2 * HEAD_DIM, tq), BF16),
                        pltpu.VMEM((GROUPS, tk, tq), F32),
                        pltpu.VMEM((GROUPS, 1, tq), F32),
                        pltpu.VMEM((GROUPS, V_ROWS, tq), F32)],
        compiler_params=_params("parallel", "parallel", "parallel"),
        name="attn",
    )(qt, k4, vt4, gain_t)


def _group_mean(sq, gmat):
    hi = sq.astype(BF16)
    lo = (sq - hi.astype(F32)).astype(BF16)
    return (jnp.dot(hi, gmat, preferred_element_type=F32) +
            jnp.dot(lo, gmat, preferred_element_type=F32))


def _convmod_kernel(prev_ref, cur_ref, next_ref, dw_ref, vec_ref, pw_ref, gmat_ref, o_ref, ext_ref, z_ref):
    i = pl.program_id(1)
    last = pl.num_programs(1) - 1
    t = cur_ref.shape[0]
    ext_ref[0:HALO, :] = jnp.where(i > 0, prev_ref[...], 0.0)
    ext_ref[HALO:HALO + t, :] = cur_ref[...]
    ext_ref[HALO + t:, :] = jnp.where(i < last, next_ref[...], 0.0)

    dw_b, ln_g, ln_b, out_g = vec_ref[0:1, :], vec_ref[1:2, :], vec_ref[2:3, :], vec_ref[3:4, :]
    base = HALO - CONV_PAD
    for r in range(0, t, CONV_ROWS):
        acc = jnp.broadcast_to(dw_b, (CONV_ROWS, dw_b.shape[1]))
        for j in range(CONV_WIDTH):
            acc = acc + dw_ref[j:j + 1, :] * ext_ref[base + r + j:base + r + j + CONV_ROWS, :]
        y = _ln_noaffine(acc) * ln_g + ln_b
        z_ref[r:r + CONV_ROWS, :] = (y * jax.nn.sigmoid(y)).astype(BF16)

    pw = jnp.dot(z_ref[...], pw_ref[...], preferred_element_type=F32)
    ms = _group_mean(pw * pw, gmat_ref[...])
    o_ref[...] = (pw * lax.rsqrt(ms + EPS) * out_g).astype(BF16)


def _convmod(h, dw_w, vecs, pw_bf, gmat):
    b, s, ch = h.shape
    t = ROW_BLOCK
    hb = t // HALO
    n_halo = s // HALO
    return pl.pallas_call(
        _convmod_kernel,
        out_shape=jax.ShapeDtypeStruct((b, s, ch), BF16),
        grid=(b, s // t),
        in_specs=[pl.BlockSpec((None, HALO, ch), lambda bi, i: (bi, jnp.maximum(i * hb - 1, 0), 0)),
                  pl.BlockSpec((None, t, ch), lambda bi, i: (bi, i, 0)),
                  pl.BlockSpec((None, HALO, ch), lambda bi, i: (bi, jnp.minimum((i + 1) * hb, n_halo - 1), 0)),
                  pl.BlockSpec(dw_w.shape, lambda bi, i: (0, 0)),
                  pl.BlockSpec(vecs.shape, lambda bi, i: (0, 0)),
                  pl.BlockSpec(pw_bf.shape, lambda bi, i: (0, 0)),
                  pl.BlockSpec(gmat.shape, lambda bi, i: (0, 0))],
        out_specs=pl.BlockSpec((None, t, ch), lambda bi, i: (bi, i, 0)),
        scratch_shapes=[pltpu.VMEM((t + 2 * HALO, ch), F32),
                        pltpu.VMEM((t, ch), BF16)],
        compiler_params=_params("parallel", "parallel"),
        name="convmod",
    )(h, h, h, dw_w, vecs, pw_bf, gmat)


def _outproj_kernel(alpha, x_ref, a_ref, c_ref, mod_ref, wa_ref, wc_ref, ln_ref, x1_ref, u2_ref):
    mixed = (jnp.dot(a_ref[...], wa_ref[...], preferred_element_type=F32) +
             jnp.dot(c_ref[...], wc_ref[...], preferred_element_type=F32))
    gate1, shift2, scale2 = mod_ref[2:3, :], mod_ref[3:4, :], mod_ref[4:5, :]
    y = alpha * x_ref[...] + gate1 * mixed
    x1 = _ln_noaffine(y) * ln_ref[0:1, :] + ln_ref[1:2, :]
    x1_ref[...] = x1
    u2_ref[...] = (_ln_noaffine(x1) * (1.0 + scale2) + shift2).astype(BF16)


def _outproj(alpha, x, attn, hc, mod, wo_a, wo_c, ln1):
    b, s, d = x.shape
    t = ROW_BLOCK
    aw, cw = attn.shape[2], hc.shape[2]
    return pl.pallas_call(
        functools.partial(_outproj_kernel, alpha),
        out_shape=(jax.ShapeDtypeStruct((b, s, d), F32),
                   jax.ShapeDtypeStruct((b, s, d), BF16)),
        grid=(b, s // t),
        in_specs=[pl.BlockSpec((None, t, d), lambda bi, i: (bi, i, 0)),
                  pl.BlockSpec((None, t, aw), lambda bi, i: (bi, i, 0)),
                  pl.BlockSpec((None, t, cw), lambda bi, i: (bi, i, 0)),
                  pl.BlockSpec((None, 6, d), lambda bi, i: (bi, 0, 0)),
                  pl.BlockSpec((aw, d), lambda bi, i: (0, 0)),
                  pl.BlockSpec((cw, d), lambda bi, i: (0, 0)),
                  pl.BlockSpec((2, d), lambda bi, i: (0, 0))],
        out_specs=(pl.BlockSpec((None, t, d), lambda bi, i: (bi, i, 0)),
                   pl.BlockSpec((None, t, d), lambda bi, i: (bi, i, 0))),
        compiler_params=_params("parallel", "parallel"),
        name="outproj",
    )(x, attn, hc, mod, wo_a, wo_c, ln1)


def _ffn_kernel(alpha, prev_ref, cur_ref, next_ref, x1_ref, mod_ref, wup_ref, dw_ref, wdn_ref, ln_ref,
                o_ref, ext_ref, hv_ref, hg_ref):
    i = pl.program_id(1)
    last = pl.num_programs(1) - 1
    t = cur_ref.shape[0]
    d_ff = wdn_ref.shape[0]
    ext_ref[0:HALO, :] = jnp.where(i > 0, prev_ref[...], jnp.zeros_like(prev_ref[...]))
    ext_ref[HALO:HALO + t, :] = cur_ref[...]
    ext_ref[HALO + t:, :] = jnp.where(i < last, next_ref[...], jnp.zeros_like(next_ref[...]))
    ext = ext_ref[...]

    def conv3(h_ref, col):
        w = dw_ref[:, col:col + FF_CHUNK]
        return (w[0:1] * h_ref[HALO - 1:HALO - 1 + t, :] + w[1:2] * h_ref[HALO:HALO + t, :] +
                w[2:3] * h_ref[HALO + 1:HALO + 1 + t, :] + w[3:4])

    acc = jnp.zeros((t, o_ref.shape[1]), F32)
    for c in range(0, d_ff, FF_CHUNK):
        hv_ref[...] = jnp.dot(ext, wup_ref[:, c:c + FF_CHUNK], preferred_element_type=F32)
        hg_ref[...] = jnp.dot(ext, wup_ref[:, d_ff + c:d_ff + c + FF_CHUNK], preferred_element_type=F32)
        val = conv3(hv_ref, c)
        gt = conv3(hg_ref, d_ff + c)
        act = 0.5 * gt * (1.0 + lax.erf(gt * (2.0 ** -0.5))) * val
        acc = acc + jnp.dot(act.astype(BF16), wdn_ref[c:c + FF_CHUNK, :], preferred_element_type=F32)

    gate2 = mod_ref[5:6, :]
    y = alpha * x1_ref[...] + gate2 * acc
    o_ref[...] = _ln_noaffine(y) * ln_ref[0:1, :] + ln_ref[1:2, :]


def _ffn(alpha, u2, x1, mod, wup_bf, dw4, wdn_bf, ln2):
    b, s, d = x1.shape
    t = ROW_BLOCK
    hb = t // HALO
    n_halo = s // HALO
    resident = dict(pipeline_mode=pl.Buffered(1))
    return pl.pallas_call(
        functools.partial(_ffn_kernel, alpha),
        out_shape=jax.ShapeDtypeStruct((b, s, d), F32),
        grid=(b, s // t),
        in_specs=[pl.BlockSpec((None, HALO, d), lambda bi, i: (bi, jnp.maximum(i * hb - 1, 0), 0)),
                  pl.BlockSpec((None, t, d), lambda bi, i: (bi, i, 0)),
                  pl.BlockSpec((None, HALO, d), lambda bi, i: (bi, jnp.minimum((i + 1) * hb, n_halo - 1), 0)),
                  pl.BlockSpec((None, t, d), lambda bi, i: (bi, i, 0)),
                  pl.BlockSpec((None, 6, d), lambda bi, i: (bi, 0, 0)),
                  pl.BlockSpec(wup_bf.shape, lambda bi, i: (0, 0), **resident),
                  pl.BlockSpec(dw4.shape, lambda bi, i: (0, 0)),
                  pl.BlockSpec(wdn_bf.shape, lambda bi, i: (0, 0), **resident),
                  pl.BlockSpec((2, d), lambda bi, i: (0, 0))],
        out_specs=pl.BlockSpec((None, t, d), lambda bi, i: (bi, i, 0)),
        scratch_shapes=[pltpu.VMEM((t + 2 * HALO, d), BF16),
                        pltpu.VMEM((t + 2 * HALO, FF_CHUNK), F32),
                        pltpu.VMEM((t + 2 * HALO, FF_CHUNK), F32)],
        compiler_params=_params("parallel", "parallel"),
        name="ffn",
    )(u2, u2, u2, x1, mod, wup_bf, dw4, wdn_bf, ln2)


def _rope_table_t(seq_len):
    pos = jnp.arange(seq_len, dtype=jnp.int32)
    rows = (pos // GRID_W).astype(F32)
    cols = (pos % GRID_W).astype(F32)
    inv_freq = ROPE_THETA ** (-jnp.arange(ROPE_FREQS, dtype=F32) / ROPE_FREQS)
    ang_r = inv_freq[:, None] * rows[None, :]
    ang_c = inv_freq[:, None] * cols[None, :]
    return jnp.concatenate([jnp.cos(ang_r), jnp.sin(ang_r), jnp.cos(ang_c), jnp.sin(ang_c)], axis=0)


def kernel(x, c, w_ada, b_ada, w_in, q_norm_g, k_norm_g, conv_dw_w, conv_dw_b, conv_ln_g, conv_ln_b,
           w_conv_pw2, attn_out_g, conv_out_g, w_o, ln1_g, ln1_b, w_up, ffn_dw_w, ffn_dw_b, w_down,
           ln2_g, ln2_b):
    b, s, d = x.shape
    depth = w_ada.shape[0]
    alpha = (2.0 * depth) ** 0.25
    conv_ch = w_conv_pw2.shape[1]
    assert s % ROW_BLOCK == 0 and s % Q_BLOCK == 0 and s % GRID_W == 0
    assert w_down.shape[1] % FF_CHUNK == 0 and b <= 8

    tab = _rope_table_t(s)
    c_pad = jnp.zeros((8, d), F32).at[:b].set(c)
    gidx = jnp.arange(conv_ch) // CONV_GROUP_DIM
    gmat = jnp.where(gidx[:, None] == gidx[None, :], 1.0 / CONV_GROUP_DIM, 0.0).astype(BF16)

    for l in range(depth):
        mod = _adaln(c_pad, w_ada[l], b_ada[l][None, :])[:b].reshape(b, 6, d)

        gains = jnp.stack([jnp.broadcast_to(q_norm_g[l][:, None], (HEAD_DIM, ROW_BLOCK)),
                           jnp.broadcast_to(k_norm_g[l][:, None], (HEAD_DIM, ROW_BLOCK))])
        qt, k, vt4, h = _inproj(x, mod, w_in[l].astype(BF16), gains, tab)

        k4 = k.reshape(b, s // KV_CHUNK, KV_CHUNK, KV_WIDTH)
        gain_t = jnp.broadcast_to(attn_out_g[l].reshape(N_KV_HEADS, GROUPS * HEAD_DIM, 1),
                                  (N_KV_HEADS, GROUPS * HEAD_DIM, Q_BLOCK))
        attn = _attention(qt, k4, vt4, gain_t)

        dw_w = jnp.zeros((CONV_WIDTH + 1, conv_ch), F32).at[:CONV_WIDTH].set(conv_dw_w[l])
        vecs = jnp.stack([conv_dw_b[l], conv_ln_g[l], conv_ln_b[l], conv_out_g[l].reshape(-1)])
        hc = _convmod(h, dw_w, vecs, w_conv_pw2[l].astype(BF16), gmat)

        wo_bf = w_o[l].astype(BF16)
        x1, u2 = _outproj(alpha, x, attn, hc, mod, wo_bf[:ATTN_WIDTH], wo_bf[ATTN_WIDTH:],
                          jnp.stack([ln1_g[l], ln1_b[l]]))

        dw4 = jnp.concatenate([ffn_dw_w[l], ffn_dw_b[l][None, :]], axis=0)
        x = _ffn(alpha, u2, x1, mod, w_up[l].astype(BF16), dw4, w_down[l].astype(BF16),
                 jnp.stack([ln2_g[l], ln2_b[l]]))
    return x
```

```python
import functools

import jax
import jax.numpy as jnp
from jax import lax
from jax.experimental import pallas as pl
from jax.experimental.pallas import tpu as pltpu

GRID_W = 64
HEAD_DIM = 64
N_HEADS = 8
N_KV_HEADS = 2
GROUPS = N_HEADS // N_KV_HEADS
ATTN_WIDTH = N_HEADS * HEAD_DIM
KV_WIDTH = N_KV_HEADS * HEAD_DIM
CONV_GROUP_DIM = 64
CONV_WIDTH = 31
CONV_PAD = (CONV_WIDTH - 1) // 2
FFN_CONV_WIDTH = 3
ROPE_THETA = 10000.0
ROPE_FREQS = HEAD_DIM // 4
EPS = 1e-6
Q_SCALE = HEAD_DIM ** -0.5 * 1.4426950408889634
SUM_ROWS = 16
V_ROWS = HEAD_DIM + SUM_ROWS

F32 = jnp.float32
BF16 = jnp.bfloat16

V7X_VMEM_LIMIT_BYTES = 56 * 1024 * 1024
ROW_BLOCK = 512
Q_BLOCK = 512
KV_CHUNK = ROW_BLOCK
HALO = 16
CONV_ROWS = 64
FF_CHUNK = 256
MAX_SLAB = 64
SCORES_AHEAD = 2
CHUNK_UNROLL = 2


def _params(*sem):
    return pltpu.CompilerParams(dimension_semantics=sem, vmem_limit_bytes=V7X_VMEM_LIMIT_BYTES)


def _ln_noaffine(x):
    mu = jnp.mean(x, axis=-1, keepdims=True)
    xc = x - mu
    var = jnp.mean(xc * xc, axis=-1, keepdims=True)
    return xc * lax.rsqrt(var + EPS)


def _adaln_kernel(c_ref, w_ref, b_ref, o_ref):
    c = c_ref[...]
    c_act = c * jax.nn.sigmoid(c)
    w = w_ref[...]
    c_hi, w_hi = c_act.astype(BF16), w.astype(BF16)
    c_lo = (c_act - c_hi.astype(F32)).astype(BF16)
    w_lo = (w - w_hi.astype(F32)).astype(BF16)
    o_ref[...] = (jnp.dot(c_hi, w_hi, preferred_element_type=F32) +
                  jnp.dot(c_hi, w_lo, preferred_element_type=F32) +
                  jnp.dot(c_lo, w_hi, preferred_element_type=F32)) + b_ref[...]


def _adaln(c_pad, w, b):
    rows, d = c_pad.shape
    n = w.shape[1]
    return pl.pallas_call(
        _adaln_kernel,
        out_shape=jax.ShapeDtypeStruct((rows, n), F32),
        grid=(n // d,),
        in_specs=[pl.BlockSpec((rows, d), lambda j: (0, 0)),
                  pl.BlockSpec((d, d), lambda j: (0, j)),
                  pl.BlockSpec((1, d), lambda j: (0, j))],
        out_specs=pl.BlockSpec((rows, d), lambda j: (0, j)),
        compiler_params=_params("arbitrary"),
        name="adaln",
    )(c_pad, w, b)


def _rope_t(x, tab):
    f = ROPE_FREQS
    cr, sr, cc, sc = tab[0:f], tab[f:2 * f], tab[2 * f:3 * f], tab[3 * f:4 * f]
    x1r, x2r, x1c, x2c = x[0:f], x[f:2 * f], x[2 * f:3 * f], x[3 * f:4 * f]
    return jnp.concatenate([x1r * cr - x2r * sr, x2r * cr + x1r * sr,
                            x1c * cc - x2c * sc, x2c * cc + x1c * sc], axis=0)


def _norm_rope_t(xt, gain, tab):
    ms = jnp.mean(xt * xt, axis=0, keepdims=True)
    return _rope_t(xt * lax.rsqrt(ms + EPS) * gain, tab)


def _inproj_kernel(x_ref, mod_ref, w_ref, gains_ref, tab_ref, qt_ref, k_ref, vt_ref, h_ref):
    x = x_ref[...]
    shift, scale = mod_ref[0:1, :], mod_ref[1:2, :]
    u = _ln_noaffine(x) * (1.0 + scale) + shift
    proj = jnp.dot(u.astype(BF16), w_ref[...], preferred_element_type=F32)

    qkv_w = ATTN_WIDTH + 2 * KV_WIDTH
    qkv_t = proj[:, :qkv_w].T
    tab = tab_ref[...]
    gq, gk = gains_ref[0], gains_ref[1]
    for h in range(N_HEADS):
        r = h * HEAD_DIM
        qh = _norm_rope_t(qkv_t[r:r + HEAD_DIM], gq, tab) * Q_SCALE
        qt_ref[r:r + HEAD_DIM, :] = qh.astype(BF16)
    k_rot = []
    for h in range(N_KV_HEADS):
        r = ATTN_WIDTH + h * HEAD_DIM
        k_rot.append(_norm_rope_t(qkv_t[r:r + HEAD_DIM], gk, tab))
    k_ref[...] = jnp.concatenate(k_rot, axis=0).T.astype(BF16)
    for h in range(N_KV_HEADS):
        r = ATTN_WIDTH + KV_WIDTH + h * HEAD_DIM
        vt_ref[h, 0:HEAD_DIM, :] = qkv_t[r:r + HEAD_DIM].astype(BF16)
        vt_ref[h, HEAD_DIM:, :] = jnp.ones((SUM_ROWS, vt_ref.shape[2]), BF16)

    conv_ch = (proj.shape[1] - qkv_w) // 2
    a = proj[:, qkv_w:qkv_w + conv_ch]
    g = proj[:, qkv_w + conv_ch:]
    h_ref[...] = a * jax.nn.sigmoid(g)


def _inproj(x, mod, w_bf, gains, tab):
    b, s, d = x.shape
    t = ROW_BLOCK
    n_cols = w_bf.shape[1]
    conv_ch = (n_cols - ATTN_WIDTH - 2 * KV_WIDTH) // 2
    return pl.pallas_call(
        _inproj_kernel,
        out_shape=(jax.ShapeDtypeStruct((b, ATTN_WIDTH, s), BF16),
                   jax.ShapeDtypeStruct((b, s, KV_WIDTH), BF16),
                   jax.ShapeDtypeStruct((b, s // t, N_KV_HEADS, V_ROWS, t), BF16),
                   jax.ShapeDtypeStruct((b, s, conv_ch), F32)),
        grid=(b, s // t),
        in_specs=[pl.BlockSpec((None, t, d), lambda bi, i: (bi, i, 0)),
                  pl.BlockSpec((None, 6, d), lambda bi, i: (bi, 0, 0)),
                  pl.BlockSpec((d, n_cols), lambda bi, i: (0, 0)),
                  pl.BlockSpec((2, HEAD_DIM, t), lambda bi, i: (0, 0, 0)),
                  pl.BlockSpec((HEAD_DIM, t), lambda bi, i: (0, i))],
        out_specs=(pl.BlockSpec((None, ATTN_WIDTH, t), lambda bi, i: (bi, 0, i)),
                   pl.BlockSpec((None, t, KV_WIDTH), lambda bi, i: (bi, i, 0)),
                   pl.BlockSpec((None, None, N_KV_HEADS, V_ROWS, t), lambda bi, i: (bi, i, 0, 0, 0)),
                   pl.BlockSpec((None, t, conv_ch), lambda bi, i: (bi, i, 0))),
        compiler_params=_params("parallel", "parallel"),
        name="inproj",
    )(x, mod, w_bf, gains, tab)


def _attn_kernel(qt_ref, k_ref, vt_ref, g_ref, o_ref, qp_ref, s_ref, m_ref, acc_ref):
    grp = pl.program_id(1)
    n_chunks, tk = k_ref.shape[0], k_ref.shape[1]
    tq = qt_ref.shape[1]

    first = grp == 0
    for h in range(GROUPS):
        qh = qt_ref[h * HEAD_DIM:(h + 1) * HEAD_DIM, :]
        z = jnp.zeros_like(qh)
        qp_ref[h, 0:HEAD_DIM, :] = jnp.where(first, qh, z)
        qp_ref[h, HEAD_DIM:2 * HEAD_DIM, :] = jnp.where(first, z, qh)
    m_ref[...] = jnp.full(m_ref.shape, -jnp.inf, F32)
    acc_ref[...] = jnp.zeros(acc_ref.shape, F32)

    def scores(c, h):
        return jnp.dot(k_ref[c], qp_ref[h], preferred_element_type=F32)

    for h in range(SCORES_AHEAD):
        s_ref[h] = scores(0, h)

    def chunk(c, carry):
        vc = vt_ref[c]
        c_next = jnp.minimum(c + 1, n_chunks - 1)
        for h in range(GROUPS):
            ahead = h + SCORES_AHEAD
            if ahead < GROUPS:
                s_ref[ahead] = scores(c, ahead)
            else:
                s_ref[ahead - GROUPS] = scores(c_next, ahead - GROUPS)
            s = s_ref[h]
            m_old = m_ref[h]
            slab_max = jnp.max(s.reshape(tk // MAX_SLAB, MAX_SLAB, tq), axis=0)
            m_new = jnp.maximum(m_old, jnp.max(slab_max, axis=0, keepdims=True))
            p = jnp.exp2(s - m_new)
            alpha = jnp.exp2(m_old - m_new)
            pv = jnp.dot(vc, p.astype(BF16), preferred_element_type=F32)
            acc_ref[h] = alpha * acc_ref[h] + pv
            m_ref[h] = m_new
        return carry

    lax.fori_loop(0, n_chunks, chunk, 0, unroll=CHUNK_UNROLL)

    outs = []
    for h in range(GROUPS):
        o = acc_ref[h, 0:HEAD_DIM, :] / acc_ref[h, HEAD_DIM:HEAD_DIM + 1, :]
        ms = jnp.mean(o * o, axis=0, keepdims=True)
        outs.append(o * lax.rsqrt(ms + EPS) * g_ref[h * HEAD_DIM:(h + 1) * HEAD_DIM, :])
    o_ref[...] = jnp.concatenate(outs, axis=0).T.astype(BF16)
    del tq


def _attention(qt, k4, vt4, gain_t):
    b, _, s = qt.shape
    n_chunks, tk = k4.shape[1], k4.shape[2]
    tq = Q_BLOCK
    gw = GROUPS * HEAD_DIM
    return pl.pallas_call(
        _attn_kernel,
        out_shape=jax.ShapeDtypeStruct((b, s, ATTN_WIDTH), BF16),
        grid=(b, N_KV_HEADS, s // tq),
        in_specs=[pl.BlockSpec((None, gw, tq), lambda bi, g, qi: (bi, g, qi)),
                  pl.BlockSpec((None, n_chunks, tk, KV_WIDTH), lambda bi, g, qi: (bi, 0, 0, 0)),
                  pl.BlockSpec((None, n_chunks, None, V_ROWS, tk), lambda bi, g, qi: (bi, 0, g, 0, 0)),
                  pl.BlockSpec((None, gw, tq), lambda bi, g, qi: (g, 0, 0))],
        out_specs=pl.BlockSpec((None, tq, gw), lambda bi, g, qi: (bi, qi, g)),
        scratch_shapes=[pltpu.VMEM((GROUPS, 2 * HEAD_DIM, tq), BF16),
                        pltpu.VMEM((GROUPS, tk, tq), F32),
                        pltpu.VMEM((GROUPS, 1, tq), F32),
                        pltpu.VMEM((GROUPS, V_ROWS, tq), F32)],
        compiler_params=_params("parallel", "parallel", "parallel"),
        name="attn",
    )(qt, k4, vt4, gain_t)


def _group_mean(sq, gmat):
    hi = sq.astype(BF16)
    lo = (sq - hi.astype(F32)).astype(BF16)
    return (jnp.dot(hi, gmat, preferred_element_type=F32) +
            jnp.dot(lo, gmat, preferred_element_type=F32))


def _convmod_kernel(prev_ref, cur_ref, next_ref, dw_ref, vec_ref, pw_ref, gmat_ref, o_ref, ext_ref, z_ref):
    i = pl.program_id(1)
    last = pl.num_programs(1) - 1
    t = cur_ref.shape[0]
    ext_ref[0:HALO, :] = jnp.where(i > 0, prev_ref[...], 0.0)
    ext_ref[HALO:HALO + t, :] = cur_ref[...]
    ext_ref[HALO + t:, :] = jnp.where(i < last, next_ref[...], 0.0)

    dw_b, ln_g, ln_b, out_g = vec_ref[0:1, :], vec_ref[1:2, :], vec_ref[2:3, :], vec_ref[3:4, :]
    base = HALO - CONV_PAD
    sub = 8
    for r in range(0, t, CONV_ROWS):
        acc = jnp.broadcast_to(dw_b, (CONV_ROWS, dw_b.shape[1]))
        for s in range(sub):
            part = None
            for j in range(CONV_WIDTH):
                if (j + base) % sub != s:
                    continue
                lo = r + (j + base) - s
                term = dw_ref[j:j + 1, :] * ext_ref[lo:lo + CONV_ROWS + sub, :]
                part = term if part is None else part + term
            acc = acc + part[s:s + CONV_ROWS, :]
        y = _ln_noaffine(acc) * ln_g + ln_b
        z_ref[r:r + CONV_ROWS, :] = (y * jax.nn.sigmoid(y)).astype(BF16)

    pw = jnp.dot(z_ref[...], pw_ref[...], preferred_element_type=F32)
    ms = _group_mean(pw * pw, gmat_ref[...])
    o_ref[...] = (pw * lax.rsqrt(ms + EPS) * out_g).astype(BF16)


def _convmod(h, dw_w, vecs, pw_bf, gmat):
    b, s, ch = h.shape
    t = ROW_BLOCK
    hb = t // HALO
    n_halo = s // HALO
    return pl.pallas_call(
        _convmod_kernel,
        out_shape=jax.ShapeDtypeStruct((b, s, ch), BF16),
        grid=(b, s // t),
        in_specs=[pl.BlockSpec((None, HALO, ch), lambda bi, i: (bi, jnp.maximum(i * hb - 1, 0), 0)),
                  pl.BlockSpec((None, t, ch), lambda bi, i: (bi, i, 0)),
                  pl.BlockSpec((None, HALO, ch), lambda bi, i: (bi, jnp.minimum((i + 1) * hb, n_halo - 1), 0)),
                  pl.BlockSpec(dw_w.shape, lambda bi, i: (0, 0)),
                  pl.BlockSpec(vecs.shape, lambda bi, i: (0, 0)),
                  pl.BlockSpec(pw_bf.shape, lambda bi, i: (0, 0)),
                  pl.BlockSpec(gmat.shape, lambda bi, i: (0, 0))],
        out_specs=pl.BlockSpec((None, t, ch), lambda bi, i: (bi, i, 0)),
        scratch_shapes=[pltpu.VMEM((t + 2 * HALO, ch), F32),
                        pltpu.VMEM((t, ch), BF16)],
        compiler_params=_params("parallel", "parallel"),
        name="convmod",
    )(h, h, h, dw_w, vecs, pw_bf, gmat)


def _outproj_kernel(alpha, x_ref, a_ref, c_ref, mod_ref, wa_ref, wc_ref, ln_ref, x1_ref, u2_ref):
    mixed = (jnp.dot(a_ref[...], wa_ref[...], preferred_element_type=F32) +
             jnp.dot(c_ref[...], wc_ref[...], preferred_element_type=F32))
    gate1, shift2, scale2 = mod_ref[2:3, :], mod_ref[3:4, :], mod_ref[4:5, :]
    y = alpha * x_ref[...] + gate1 * mixed
    x1 = _ln_noaffine(y) * ln_ref[0:1, :] + ln_ref[1:2, :]
    x1_ref[...] = x1
    u2_ref[...] = (_ln_noaffine(x1) * (1.0 + scale2) + shift2).astype(BF16)


def _outproj(alpha, x, attn, hc, mod, wo_a, wo_c, ln1):
    b, s, d = x.shape
    t = ROW_BLOCK
    aw, cw = attn.shape[2], hc.shape[2]
    return pl.pallas_call(
        functools.partial(_outproj_kernel, alpha),
        out_shape=(jax.ShapeDtypeStruct((b, s, d), F32),
                   jax.ShapeDtypeStruct((b, s, d), BF16)),
        grid=(b, s // t),
        in_specs=[pl.BlockSpec((None, t, d), lambda bi, i: (bi, i, 0)),
                  pl.BlockSpec((None, t, aw), lambda bi, i: (bi, i, 0)),
                  pl.BlockSpec((None, t, cw), lambda bi, i: (bi, i, 0)),
                  pl.BlockSpec((None, 6, d), lambda bi, i: (bi, 0, 0)),
                  pl.BlockSpec((aw, d), lambda bi, i: (0, 0)),
                  pl.BlockSpec((cw, d), lambda bi, i: (0, 0)),
                  pl.BlockSpec((2, d), lambda bi, i: (0, 0))],
        out_specs=(pl.BlockSpec((None, t, d), lambda bi, i: (bi, i, 0)),
                   pl.BlockSpec((None, t, d), lambda bi, i: (bi, i, 0))),
        compiler_params=_params("parallel", "parallel"),
        name="outproj",
    )(x, attn, hc, mod, wo_a, wo_c, ln1)


def _ffn_kernel(alpha, prev_ref, cur_ref, next_ref, x1_ref, mod_ref, wup_ref, dw_ref, wdn_ref, ln_ref,
                o_ref, ext_ref, hv_ref, hg_ref):
    i = pl.program_id(1)
    last = pl.num_programs(1) - 1
    t = cur_ref.shape[0]
    d_ff = wdn_ref.shape[0]
    ext_ref[0:HALO, :] = jnp.where(i > 0, prev_ref[...], jnp.zeros_like(prev_ref[...]))
    ext_ref[HALO:HALO + t, :] = cur_ref[...]
    ext_ref[HALO + t:, :] = jnp.where(i < last, next_ref[...], jnp.zeros_like(next_ref[...]))
    ext = ext_ref[...]

    def conv3(h_ref, slot, col):
        w = dw_ref[:, col:col + FF_CHUNK]
        return (w[0:1] * h_ref[slot, HALO - 1:HALO - 1 + t, :] + w[1:2] * h_ref[slot, HALO:HALO + t, :] +
                w[2:3] * h_ref[slot, HALO + 1:HALO + 1 + t, :] + w[3:4])

    def up(c, slot):
        hv_ref[slot] = jnp.dot(ext, wup_ref[:, c:c + FF_CHUNK], preferred_element_type=F32)
        hg_ref[slot] = jnp.dot(ext, wup_ref[:, d_ff + c:d_ff + c + FF_CHUNK], preferred_element_type=F32)

    up(0, 0)
    acc = jnp.zeros((t, o_ref.shape[1]), F32)
    for n, c in enumerate(range(0, d_ff, FF_CHUNK)):
        slot = n % 2
        if c + FF_CHUNK < d_ff:
            up(c + FF_CHUNK, 1 - slot)
        val = conv3(hv_ref, slot, c)
        gt = conv3(hg_ref, slot, d_ff + c)
        act = 0.5 * gt * (1.0 + lax.erf(gt * (2.0 ** -0.5))) * val
        acc = acc + jnp.dot(act.astype(BF16), wdn_ref[c:c + FF_CHUNK, :], preferred_element_type=F32)

    gate2 = mod_ref[5:6, :]
    y = alpha * x1_ref[...] + gate2 * acc
    o_ref[...] = _ln_noaffine(y) * ln_ref[0:1, :] + ln_ref[1:2, :]


def _ffn(alpha, u2, x1, mod, wup_bf, dw4, wdn_bf, ln2):
    b, s, d = x1.shape
    t = ROW_BLOCK
    hb = t // HALO
    n_halo = s // HALO
    resident = dict(pipeline_mode=pl.Buffered(1))
    return pl.pallas_call(
        functools.partial(_ffn_kernel, alpha),
        out_shape=jax.ShapeDtypeStruct((b, s, d), F32),
        grid=(b, s // t),
        in_specs=[pl.BlockSpec((None, HALO, d), lambda bi, i: (bi, jnp.maximum(i * hb - 1, 0), 0)),
                  pl.BlockSpec((None, t, d), lambda bi, i: (bi, i, 0)),
                  pl.BlockSpec((None, HALO, d), lambda bi, i: (bi, jnp.minimum((i + 1) * hb, n_halo - 1), 0)),
                  pl.BlockSpec((None, t, d), lambda bi, i: (bi, i, 0)),
                  pl.BlockSpec((None, 6, d), lambda bi, i: (bi, 0, 0)),
                  pl.BlockSpec(wup_bf.shape, lambda bi, i: (0, 0), **resident),
                  pl.BlockSpec(dw4.shape, lambda bi, i: (0, 0)),
                  pl.BlockSpec(wdn_bf.shape, lambda bi, i: (0, 0), **resident),
                  pl.BlockSpec((2, d), lambda bi, i: (0, 0))],
        out_specs=pl.BlockSpec((None, t, d), lambda bi, i: (bi, i, 0)),
        scratch_shapes=[pltpu.VMEM((t + 2 * HALO, d), BF16),
                        pltpu.VMEM((2, t + 2 * HALO, FF_CHUNK), F32),
                        pltpu.VMEM((2, t + 2 * HALO, FF_CHUNK), F32)],
        compiler_params=_params("parallel", "parallel"),
        name="ffn",
    )(u2, u2, u2, x1, mod, wup_bf, dw4, wdn_bf, ln2)


def _rope_table_t(seq_len):
    pos = jnp.arange(seq_len, dtype=jnp.int32)
    rows = (pos // GRID_W).astype(F32)
    cols = (pos % GRID_W).astype(F32)
    inv_freq = ROPE_THETA ** (-jnp.arange(ROPE_FREQS, dtype=F32) / ROPE_FREQS)
    ang_r = inv_freq[:, None] * rows[None, :]
    ang_c = inv_freq[:, None] * cols[None, :]
    return jnp.concatenate([jnp.cos(ang_r), jnp.sin(ang_r), jnp.cos(ang_c), jnp.sin(ang_c)], axis=0)


def kernel(x, c, w_ada, b_ada, w_in, q_norm_g, k_norm_g, conv_dw_w, conv_dw_b, conv_ln_g, conv_ln_b,
           w_conv_pw2, attn_out_g, conv_out_g, w_o, ln1_g, ln1_b, w_up, ffn_dw_w, ffn_dw_b, w_down,
           ln2_g, ln2_b):
    b, s, d = x.shape
    depth = w_ada.shape[0]
    alpha = (2.0 * depth) ** 0.25
    conv_ch = w_conv_pw2.shape[1]
    assert s % ROW_BLOCK == 0 and s % Q_BLOCK == 0 and s % GRID_W == 0
    assert w_down.shape[1] % FF_CHUNK == 0 and b <= 8

    tab = _rope_table_t(s)
    c_pad = jnp.zeros((8, d), F32).at[:b].set(c)
    gidx = jnp.arange(conv_ch) // CONV_GROUP_DIM
    gmat = jnp.where(gidx[:, None] == gidx[None, :], 1.0 / CONV_GROUP_DIM, 0.0).astype(BF16)

    for l in range(depth):
        mod = _adaln(c_pad, w_ada[l], b_ada[l][None, :])[:b].reshape(b, 6, d)

        gains = jnp.stack([jnp.broadcast_to(q_norm_g[l][:, None], (HEAD_DIM, ROW_BLOCK)),
                           jnp.broadcast_to(k_norm_g[l][:, None], (HEAD_DIM, ROW_BLOCK))])
        qt, k, vt4, h = _inproj(x, mod, w_in[l].astype(BF16), gains, tab)

        k4 = k.reshape(b, s // KV_CHUNK, KV_CHUNK, KV_WIDTH)
        gain_t = jnp.broadcast_to(attn_out_g[l].reshape(N_KV_HEADS, GROUPS * HEAD_DIM, 1),
                                  (N_KV_HEADS, GROUPS * HEAD_DIM, Q_BLOCK))
        attn = _attention(qt, k4, vt4, gain_t)

        dw_w = jnp.zeros((CONV_WIDTH + 1, conv_ch), F32).at[:CONV_WIDTH].set(conv_dw_w[l])
        vecs = jnp.stack([conv_dw_b[l], conv_ln_g[l], conv_ln_b[l], conv_out_g[l].reshape(-1)])
        hc = _convmod(h, dw_w, vecs, w_conv_pw2[l].astype(BF16), gmat)

        wo_bf = w_o[l].astype(BF16)
        x1, u2 = _outproj(alpha, x, attn, hc, mod, wo_bf[:ATTN_WIDTH], wo_bf[ATTN_WIDTH:],
                          jnp.stack([ln1_g[l], ln1_b[l]]))

        dw4 = jnp.concatenate([ffn_dw_w[l], ffn_dw_b[l][None, :]], axis=0)
        x = _ffn(alpha, u2, x1, mod, w_up[l].astype(BF16), dw4, w_down[l].astype(BF16),
                 jnp.stack([ln2_g[l], ln2_b[l]]))
    return x
```

```python
import functools

import jax
import jax.numpy as jnp
from jax import lax
from jax.experimental import pallas as pl
from jax.experimental.pallas import tpu as pltpu

GRID_W = 64
HEAD_DIM = 64
N_HEADS = 8
N_KV_HEADS = 2
GROUPS = N_HEADS // N_KV_HEADS
ATTN_WIDTH = N_HEADS * HEAD_DIM
KV_WIDTH = N_KV_HEADS * HEAD_DIM
CONV_GROUP_DIM = 64
CONV_WIDTH = 31
CONV_PAD = (CONV_WIDTH - 1) // 2
FFN_CONV_WIDTH = 3
ROPE_THETA = 10000.0
ROPE_FREQS = HEAD_DIM // 4
EPS = 1e-6
Q_SCALE = HEAD_DIM ** -0.5 * 1.4426950408889634
SUM_ROWS = 16
V_ROWS = HEAD_DIM + SUM_ROWS

F32 = jnp.float32
BF16 = jnp.bfloat16

V7X_VMEM_LIMIT_BYTES = 56 * 1024 * 1024
ROW_BLOCK = 512
Q_BLOCK = 512
KV_CHUNK = ROW_BLOCK
HALO = 16
CONV_ROWS = 64
FF_CHUNK = 256
NORM_ROWS = 32
MAX_SLAB = 64
SCORES_AHEAD = 2
CHUNK_UNROLL = 4
MAX_SAFE_BOUND = 60.0
BOUND_SLACK = 1.0 + 2.0 ** -6


def _params(*sem):
    return pltpu.CompilerParams(dimension_semantics=sem, vmem_limit_bytes=V7X_VMEM_LIMIT_BYTES)


def _ln_noaffine(x):
    mu = jnp.mean(x, axis=-1, keepdims=True)
    xc = x - mu
    var = jnp.mean(xc * xc, axis=-1, keepdims=True)
    return xc * lax.rsqrt(var + EPS)


def _adaln_kernel(c_ref, w_ref, b_ref, o_ref):
    c = c_ref[...]
    c_act = c * jax.nn.sigmoid(c)
    w = w_ref[...]
    c_hi, w_hi = c_act.astype(BF16), w.astype(BF16)
    c_lo = (c_act - c_hi.astype(F32)).astype(BF16)
    w_lo = (w - w_hi.astype(F32)).astype(BF16)
    o_ref[...] = (jnp.dot(c_hi, w_hi, preferred_element_type=F32) +
                  jnp.dot(c_hi, w_lo, preferred_element_type=F32) +
                  jnp.dot(c_lo, w_hi, preferred_element_type=F32)) + b_ref[...]


def _adaln(c_pad, w, b):
    rows, d = c_pad.shape
    n = w.shape[1]
    return pl.pallas_call(
        _adaln_kernel,
        out_shape=jax.ShapeDtypeStruct((rows, n), F32),
        grid=(n // d,),
        in_specs=[pl.BlockSpec((rows, d), lambda j: (0, 0)),
                  pl.BlockSpec((d, d), lambda j: (0, j)),
                  pl.BlockSpec((1, d), lambda j: (0, j))],
        out_specs=pl.BlockSpec((rows, d), lambda j: (0, j)),
        compiler_params=_params("arbitrary"),
        name="adaln",
    )(c_pad, w, b)


def _rope_t(x, tab):
    f = ROPE_FREQS
    cr, sr, cc, sc = tab[0:f], tab[f:2 * f], tab[2 * f:3 * f], tab[3 * f:4 * f]
    x1r, x2r, x1c, x2c = x[0:f], x[f:2 * f], x[2 * f:3 * f], x[3 * f:4 * f]
    return jnp.concatenate([x1r * cr - x2r * sr, x2r * cr + x1r * sr,
                            x1c * cc - x2c * sc, x2c * cc + x1c * sc], axis=0)


def _norm_rope_t(xt, gain, tab):
    ms = jnp.mean(xt * xt, axis=0, keepdims=True)
    return _rope_t(xt * lax.rsqrt(ms + EPS) * gain, tab)


def _inproj_kernel(x_ref, mod_ref, w_ref, gains_ref, tab_ref, qt_ref, k_ref, vt_ref, h_ref, kn_ref, u_ref):
    shift, scale = mod_ref[0:1, :], mod_ref[1:2, :]
    for r in range(0, x_ref.shape[0], NORM_ROWS):
        u = _ln_noaffine(x_ref[r:r + NORM_ROWS, :]) * (1.0 + scale) + shift
        u_ref[r:r + NORM_ROWS, :] = u.astype(BF16)
    proj = jnp.dot(u_ref[...], w_ref[...], preferred_element_type=F32)

    qkv_w = ATTN_WIDTH + 2 * KV_WIDTH
    qkv_t = proj[:, :qkv_w].T
    tab = tab_ref[...]
    gq, gk = gains_ref[0], gains_ref[1]
    for h in range(N_HEADS):
        r = h * HEAD_DIM
        qh = _norm_rope_t(qkv_t[r:r + HEAD_DIM], gq, tab) * Q_SCALE
        qt_ref[r:r + HEAD_DIM, :] = qh.astype(BF16)
    k_rot = []
    for h in range(N_KV_HEADS):
        r = ATTN_WIDTH + h * HEAD_DIM
        kh = _norm_rope_t(qkv_t[r:r + HEAD_DIM], gk, tab).astype(BF16)
        k_rot.append(kh)
        khf = kh.astype(F32)
        kn_ref[h:h + 1, :] = jnp.sum(khf * khf, axis=0, keepdims=True)
    k_ref[...] = jnp.concatenate(k_rot, axis=0).astype(F32).T.astype(BF16)
    for h in range(N_KV_HEADS):
        r = ATTN_WIDTH + KV_WIDTH + h * HEAD_DIM
        vt_ref[h, 0:HEAD_DIM, :] = qkv_t[r:r + HEAD_DIM].astype(BF16)
        vt_ref[h, HEAD_DIM:, :] = jnp.ones((SUM_ROWS, vt_ref.shape[2]), BF16)

    conv_ch = (proj.shape[1] - qkv_w) // 2
    a = proj[:, qkv_w:qkv_w + conv_ch]
    g = proj[:, qkv_w + conv_ch:]
    h_ref[...] = a * jax.nn.sigmoid(g)


def _inproj(x, mod, w_bf, gains, tab):
    b, s, d = x.shape
    t = ROW_BLOCK
    n_cols = w_bf.shape[1]
    conv_ch = (n_cols - ATTN_WIDTH - 2 * KV_WIDTH) // 2
    return pl.pallas_call(
        _inproj_kernel,
        out_shape=(jax.ShapeDtypeStruct((b, ATTN_WIDTH, s), BF16),
                   jax.ShapeDtypeStruct((b, s, KV_WIDTH), BF16),
                   jax.ShapeDtypeStruct((b, s // t, N_KV_HEADS, V_ROWS, t), BF16),
                   jax.ShapeDtypeStruct((b, s, conv_ch), F32),
                   jax.ShapeDtypeStruct((b, N_KV_HEADS, s), F32)),
        grid=(b, s // t),
        in_specs=[pl.BlockSpec((None, t, d), lambda bi, i: (bi, i, 0)),
                  pl.BlockSpec((None, 6, d), lambda bi, i: (bi, 0, 0)),
                  pl.BlockSpec((d, n_cols), lambda bi, i: (0, 0)),
                  pl.BlockSpec((2, HEAD_DIM, t), lambda bi, i: (0, 0, 0)),
                  pl.BlockSpec((HEAD_DIM, t), lambda bi, i: (0, i))],
        out_specs=(pl.BlockSpec((None, ATTN_WIDTH, t), lambda bi, i: (bi, 0, i)),
                   pl.BlockSpec((None, t, KV_WIDTH), lambda bi, i: (bi, i, 0)),
                   pl.BlockSpec((None, None, N_KV_HEADS, V_ROWS, t), lambda bi, i: (bi, i, 0, 0, 0)),
                   pl.BlockSpec((None, t, conv_ch), lambda bi, i: (bi, i, 0)),
                   pl.BlockSpec((None, N_KV_HEADS, t), lambda bi, i: (bi, 0, i))),
        scratch_shapes=[pltpu.VMEM((t, d), BF16)],
        compiler_params=_params("parallel", "parallel"),
        name="inproj",
    )(x, mod, w_bf, gains, tab)


def _attn_kernel(qt_ref, k_ref, vt_ref, kn_ref, g_ref, o_ref, qp_ref, s_ref, p_ref, m_ref, acc_ref):
    grp = pl.program_id(1)
    n_chunks, tk = k_ref.shape[0], k_ref.shape[1]
    tq = qt_ref.shape[1]

    first = grp == 0
    k_norm2_max = jnp.max(kn_ref[...], axis=1, keepdims=True)
    bounds = []
    for h in range(GROUPS):
        qh = qt_ref[h * HEAD_DIM:(h + 1) * HEAD_DIM, :]
        z = jnp.zeros_like(qh)
        qp_ref[h, 0:HEAD_DIM, :] = jnp.where(first, qh, z)
        qp_ref[h, HEAD_DIM:2 * HEAD_DIM, :] = jnp.where(first, z, qh)
        qf = qh.astype(F32)
        q_norm2 = jnp.sum(qf * qf, axis=0, keepdims=True)
        bounds.append(jnp.sqrt(q_norm2 * k_norm2_max) * BOUND_SLACK)
    acc_ref[...] = jnp.zeros(acc_ref.shape, F32)
    bound_max = jnp.max(jnp.concatenate(bounds, axis=0))

    def scores(c, h):
        return jnp.dot(k_ref[c], qp_ref[h], preferred_element_type=F32)

    @pl.when(bound_max <= MAX_SAFE_BOUND)
    def _():
        for h in range(GROUPS):
            m_ref[h] = bounds[h]

        def probs(c, h, slot):
            p_ref[slot] = jnp.exp2(scores(c, h) - m_ref[h]).astype(BF16)

        probs(0, 0, 0)

        def chunk(c, carry):
            vc = vt_ref[c]
            c_next = jnp.minimum(c + 1, n_chunks - 1)
            for h in range(GROUPS):
                if h + 1 < GROUPS:
                    probs(c, h + 1, (h + 1) % 2)
                else:
                    probs(c_next, 0, 0)
                acc_ref[h] += jnp.dot(vc, p_ref[h % 2], preferred_element_type=F32)
            return carry

        lax.fori_loop(0, n_chunks, chunk, 0, unroll=CHUNK_UNROLL)

    @pl.when(jnp.logical_not(bound_max <= MAX_SAFE_BOUND))
    def _():
        m_ref[...] = jnp.full(m_ref.shape, -jnp.inf, F32)
        for h in range(SCORES_AHEAD):
            s_ref[h] = scores(0, h)

        def chunk(c, carry):
            vc = vt_ref[c]
            c_next = jnp.minimum(c + 1, n_chunks - 1)
            for h in range(GROUPS):
                ahead = h + SCORES_AHEAD
                if ahead < GROUPS:
                    s_ref[ahead] = scores(c, ahead)
                else:
                    s_ref[ahead - GROUPS] = scores(c_next, ahead - GROUPS)
                s = s_ref[h]
                m_old = m_ref[h]
                slab_max = jnp.max(s.reshape(tk // MAX_SLAB, MAX_SLAB, tq), axis=0)
                m_new = jnp.maximum(m_old, jnp.max(slab_max, axis=0, keepdims=True))
                p = jnp.exp2(s - m_new)
                alpha = jnp.exp2(m_old - m_new)
                pv = jnp.dot(vc, p.astype(BF16), preferred_element_type=F32)
                acc_ref[h] = alpha * acc_ref[h] + pv
                m_ref[h] = m_new
            return carry

        lax.fori_loop(0, n_chunks, chunk, 0, unroll=2)

    outs = []
    for h in range(GROUPS):
        o = acc_ref[h, 0:HEAD_DIM, :] / acc_ref[h, HEAD_DIM:HEAD_DIM + 1, :]
        ms = jnp.mean(o * o, axis=0, keepdims=True)
        outs.append(o * lax.rsqrt(ms + EPS) * g_ref[h * HEAD_DIM:(h + 1) * HEAD_DIM, :])
    o_ref[...] = jnp.concatenate(outs, axis=0).T.astype(BF16)


def _attention(qt, k4, vt4, kn4, gain_t):
    b, _, s = qt.shape
    n_chunks, tk = k4.shape[1], k4.shape[2]
    tq = Q_BLOCK
    gw = GROUPS * HEAD_DIM
    return pl.pallas_call(
        _attn_kernel,
        out_shape=jax.ShapeDtypeStruct((b, s, ATTN_WIDTH), BF16),
        grid=(b, N_KV_HEADS, s // tq),
        in_specs=[pl.BlockSpec((None, gw, tq), lambda bi, g, qi: (bi, g, qi)),
                  pl.BlockSpec((None, n_chunks, tk, KV_WIDTH), lambda bi, g, qi: (bi, 0, 0, 0)),
                  pl.BlockSpec((None, n_chunks, None, V_ROWS, tk), lambda bi, g, qi: (bi, 0, g, 0, 0)),
                  pl.BlockSpec((None, None, 1, s), lambda bi, g, qi: (bi, g, 0, 0)),
                  pl.BlockSpec((None, gw, tq), lambda bi, g, qi: (g, 0, 0))],
        out_specs=pl.BlockSpec((None, tq, gw), lambda bi, g, qi: (bi, qi, g)),
        scratch_shapes=[pltpu.VMEM((GROUPS, 2 * HEAD_DIM, tq), BF16),
                        pltpu.VMEM((GROUPS, tk, tq), F32),
                        pltpu.VMEM((2, tk, tq), BF16),
                        pltpu.VMEM((GROUPS, 1, tq), F32),
                        pltpu.VMEM((GROUPS, V_ROWS, tq), F32)],
        compiler_params=_params("parallel", "parallel", "parallel"),
        name="attn",
    )(qt, k4, vt4, kn4, gain_t)


def _group_mean(sq, gmat):
    hi = sq.astype(BF16)
    lo = (sq - hi.astype(F32)).astype(BF16)
    return (jnp.dot(hi, gmat, preferred_element_type=F32) +
            jnp.dot(lo, gmat, preferred_element_type=F32))


def _convmod_kernel(prev_ref, cur_ref, next_ref, dw_ref, vec_ref, pw_ref, gmat_ref, o_ref, ext_ref, z_ref):
    i = pl.program_id(1)
    last = pl.num_programs(1) - 1
    t = cur_ref.shape[0]
    ext_ref[0:HALO, :] = jnp.where(i > 0, prev_ref[...], 0.0)
    ext_ref[HALO:HALO + t, :] = cur_ref[...]
    ext_ref[HALO + t:, :] = jnp.where(i < last, next_ref[...], 0.0)

    dw_b, ln_g, ln_b, out_g = vec_ref[0:1, :], vec_ref[1:2, :], vec_ref[2:3, :], vec_ref[3:4, :]
    base = HALO - CONV_PAD
    sub = 8
    for r in range(0, t, CONV_ROWS):
        acc = jnp.broadcast_to(dw_b, (CONV_ROWS, dw_b.shape[1]))
        for s in range(sub):
            part = None
            for j in range(CONV_WIDTH):
                if (j + base) % sub != s:
                    continue
                lo = r + (j + base) - s
                term = dw_ref[j:j + 1, :] * ext_ref[lo:lo + CONV_ROWS + sub, :]
                part = term if part is None else part + term
            acc = acc + part[s:s + CONV_ROWS, :]
        y = _ln_noaffine(acc) * ln_g + ln_b
        z_ref[r:r + CONV_ROWS, :] = (y * jax.nn.sigmoid(y)).astype(BF16)

    pw = jnp.dot(z_ref[...], pw_ref[...], preferred_element_type=F32)
    ms = _group_mean(pw * pw, gmat_ref[...])
    o_ref[...] = (pw * lax.rsqrt(ms + EPS) * out_g).astype(BF16)


def _convmod(h, dw_w, vecs, pw_bf, gmat):
    b, s, ch = h.shape
    t = ROW_BLOCK
    hb = t // HALO
    n_halo = s // HALO
    return pl.pallas_call(
        _convmod_kernel,
        out_shape=jax.ShapeDtypeStruct((b, s, ch), BF16),
        grid=(b, s // t),
        in_specs=[pl.BlockSpec((None, HALO, ch), lambda bi, i: (bi, jnp.maximum(i * hb - 1, 0), 0)),
                  pl.BlockSpec((None, t, ch), lambda bi, i: (bi, i, 0)),
                  pl.BlockSpec((None, HALO, ch), lambda bi, i: (bi, jnp.minimum((i + 1) * hb, n_halo - 1), 0)),
                  pl.BlockSpec(dw_w.shape, lambda bi, i: (0, 0)),
                  pl.BlockSpec(vecs.shape, lambda bi, i: (0, 0)),
                  pl.BlockSpec(pw_bf.shape, lambda bi, i: (0, 0)),
                  pl.BlockSpec(gmat.shape, lambda bi, i: (0, 0))],
        out_specs=pl.BlockSpec((None, t, ch), lambda bi, i: (bi, i, 0)),
        scratch_shapes=[pltpu.VMEM((t + 2 * HALO, ch), F32),
                        pltpu.VMEM((t, ch), BF16)],
        compiler_params=_params("parallel", "parallel"),
        name="convmod",
    )(h, h, h, dw_w, vecs, pw_bf, gmat)


def _outproj_kernel(alpha, x_ref, a_ref, c_ref, mod_ref, wa_ref, wc_ref, ln_ref, x1_ref, u2_ref, mix_ref):
    t = x_ref.shape[0]
    half = t // 2
    for r in (0, half):
        mix_ref[r:r + half, :] = (
            jnp.dot(a_ref[r:r + half, :], wa_ref[...], preferred_element_type=F32) +
            jnp.dot(c_ref[r:r + half, :], wc_ref[...], preferred_element_type=F32))
    gate1, shift2, scale2 = mod_ref[2:3, :], mod_ref[3:4, :], mod_ref[4:5, :]
    for r in range(0, t, NORM_ROWS):
        y = alpha * x_ref[r:r + NORM_ROWS, :] + gate1 * mix_ref[r:r + NORM_ROWS, :]
        x1 = _ln_noaffine(y) * ln_ref[0:1, :] + ln_ref[1:2, :]
        x1_ref[r:r + NORM_ROWS, :] = x1
        u2_ref[r:r + NORM_ROWS, :] = (_ln_noaffine(x1) * (1.0 + scale2) + shift2).astype(BF16)


def _outproj(alpha, x, attn, hc, mod, wo_a, wo_c, ln1):
    b, s, d = x.shape
    t = ROW_BLOCK
    aw, cw = attn.shape[2], hc.shape[2]
    return pl.pallas_call(
        functools.partial(_outproj_kernel, alpha),
        out_shape=(jax.ShapeDtypeStruct((b, s, d), F32),
                   jax.ShapeDtypeStruct((b, s, d), BF16)),
        grid=(b, s // t),
        in_specs=[pl.BlockSpec((None, t, d), lambda bi, i: (bi, i, 0)),
                  pl.BlockSpec((None, t, aw), lambda bi, i: (bi, i, 0)),
                  pl.BlockSpec((None, t, cw), lambda bi, i: (bi, i, 0)),
                  pl.BlockSpec((None, 6, d), lambda bi, i: (bi, 0, 0)),
                  pl.BlockSpec((aw, d), lambda bi, i: (0, 0)),
                  pl.BlockSpec((cw, d), lambda bi, i: (0, 0)),
                  pl.BlockSpec((2, d), lambda bi, i: (0, 0))],
        out_specs=(pl.BlockSpec((None, t, d), lambda bi, i: (bi, i, 0)),
                   pl.BlockSpec((None, t, d), lambda bi, i: (bi, i, 0))),
        scratch_shapes=[pltpu.VMEM((t, d), F32)],
        compiler_params=_params("parallel", "parallel"),
        name="outproj",
    )(x, attn, hc, mod, wo_a, wo_c, ln1)


def _ffn_kernel(alpha, prev_ref, cur_ref, next_ref, x1_ref, mod_ref, wup_ref, dw_ref, wdn_ref, ln_ref,
                o_ref, ext_ref, hv_ref, hg_ref, acc_ref):
    i = pl.program_id(1)
    last = pl.num_programs(1) - 1
    t = cur_ref.shape[0]
    d_ff = wdn_ref.shape[0]
    ext_ref[0:HALO, :] = jnp.where(i > 0, prev_ref[...], jnp.zeros_like(prev_ref[...]))
    ext_ref[HALO:HALO + t, :] = cur_ref[...]
    ext_ref[HALO + t:, :] = jnp.where(i < last, next_ref[...], jnp.zeros_like(next_ref[...]))
    ext = ext_ref[...]

    half = t // 2

    def conv3(h_ref, slot, col, r0):
        w = dw_ref[:, col:col + FF_CHUNK]
        lo = HALO + r0
        return (w[0:1] * h_ref[slot, lo - 1:lo - 1 + half, :] + w[1:2] * h_ref[slot, lo:lo + half, :] +
                w[2:3] * h_ref[slot, lo + 1:lo + 1 + half, :] + w[3:4])

    def up_val(c, slot):
        hv_ref[slot] = jnp.dot(ext, wup_ref[:, c:c + FF_CHUNK], preferred_element_type=F32)

    def up_gate(c, slot):
        hg_ref[slot] = jnp.dot(ext, wup_ref[:, d_ff + c:d_ff + c + FF_CHUNK], preferred_element_type=F32)

    up_val(0, 0)
    up_gate(0, 0)
    acc_ref[...] = jnp.zeros(acc_ref.shape, F32)
    for n, c in enumerate(range(0, d_ff, FF_CHUNK)):
        slot = n % 2
        for hi, up_next in enumerate((up_val, up_gate)):
            if c + FF_CHUNK < d_ff:
                up_next(c + FF_CHUNK, 1 - slot)
            r0 = hi * half
            val = conv3(hv_ref, slot, c, r0)
            gt = conv3(hg_ref, slot, d_ff + c, r0)
            act = 0.5 * gt * (1.0 + lax.erf(gt * (2.0 ** -0.5))) * val
            acc_ref[r0:r0 + half, :] += jnp.dot(act.astype(BF16), wdn_ref[c:c + FF_CHUNK, :],
                                                preferred_element_type=F32)

    gate2 = mod_ref[5:6, :]
    for r in range(0, t, NORM_ROWS):
        y = alpha * x1_ref[r:r + NORM_ROWS, :] + gate2 * acc_ref[r:r + NORM_ROWS, :]
        o_ref[r:r + NORM_ROWS, :] = _ln_noaffine(y) * ln_ref[0:1, :] + ln_ref[1:2, :]


def _ffn(alpha, u2, x1, mod, wup_bf, dw4, wdn_bf, ln2):
    b, s, d = x1.shape
    t = ROW_BLOCK
    hb = t // HALO
    n_halo = s // HALO
    resident = dict(pipeline_mode=pl.Buffered(1))
    return pl.pallas_call(
        functools.partial(_ffn_kernel, alpha),
        out_shape=jax.ShapeDtypeStruct((b, s, d), F32),
        grid=(b, s // t),
        in_specs=[pl.BlockSpec((None, HALO, d), lambda bi, i: (bi, jnp.maximum(i * hb - 1, 0), 0)),
                  pl.BlockSpec((None, t, d), lambda bi, i: (bi, i, 0)),
                  pl.BlockSpec((None, HALO, d), lambda bi, i: (bi, jnp.minimum((i + 1) * hb, n_halo - 1), 0)),
                  pl.BlockSpec((None, t, d), lambda bi, i: (bi, i, 0)),
                  pl.BlockSpec((None, 6, d), lambda bi, i: (bi, 0, 0)),
                  pl.BlockSpec(wup_bf.shape, lambda bi, i: (0, 0), **resident),
                  pl.BlockSpec(dw4.shape, lambda bi, i: (0, 0)),
                  pl.BlockSpec(wdn_bf.shape, lambda bi, i: (0, 0), **resident),
                  pl.BlockSpec((2, d), lambda bi, i: (0, 0))],
        out_specs=pl.BlockSpec((None, t, d), lambda bi, i: (bi, i, 0)),
        scratch_shapes=[pltpu.VMEM((t + 2 * HALO, d), BF16),
                        pltpu.VMEM((2, t + 2 * HALO, FF_CHUNK), F32),
                        pltpu.VMEM((2, t + 2 * HALO, FF_CHUNK), F32),
                        pltpu.VMEM((t, d), F32)],
        compiler_params=_params("parallel", "parallel"),
        name="ffn",
    )(u2, u2, u2, x1, mod, wup_bf, dw4, wdn_bf, ln2)


def _rope_table_t(seq_len):
    pos = jnp.arange(seq_len, dtype=jnp.int32)
    rows = (pos // GRID_W).astype(F32)
    cols = (pos % GRID_W).astype(F32)
    inv_freq = ROPE_THETA ** (-jnp.arange(ROPE_FREQS, dtype=F32) / ROPE_FREQS)
    ang_r = inv_freq[:, None] * rows[None, :]
    ang_c = inv_freq[:, None] * cols[None, :]
    return jnp.concatenate([jnp.cos(ang_r), jnp.sin(ang_r), jnp.cos(ang_c), jnp.sin(ang_c)], axis=0)


def kernel(x, c, w_ada, b_ada, w_in, q_norm_g, k_norm_g, conv_dw_w, conv_dw_b, conv_ln_g, conv_ln_b,
           w_conv_pw2, attn_out_g, conv_out_g, w_o, ln1_g, ln1_b, w_up, ffn_dw_w, ffn_dw_b, w_down,
           ln2_g, ln2_b):
    b, s, d = x.shape
    depth = w_ada.shape[0]
    alpha = (2.0 * depth) ** 0.25
    conv_ch = w_conv_pw2.shape[1]
    assert s % ROW_BLOCK == 0 and s % Q_BLOCK == 0 and s % GRID_W == 0
    assert w_down.shape[1] % FF_CHUNK == 0 and b <= 8

    tab = _rope_table_t(s)
    c_pad = jnp.zeros((8, d), F32).at[:b].set(c)
    gidx = jnp.arange(conv_ch) // CONV_GROUP_DIM
    gmat = jnp.where(gidx[:, None] == gidx[None, :], 1.0 / CONV_GROUP_DIM, 0.0).astype(BF16)

    for l in range(depth):
        mod = _adaln(c_pad, w_ada[l], b_ada[l][None, :])[:b].reshape(b, 6, d)

        gains = jnp.stack([jnp.broadcast_to(q_norm_g[l][:, None], (HEAD_DIM, ROW_BLOCK)),
                           jnp.broadcast_to(k_norm_g[l][:, None], (HEAD_DIM, ROW_BLOCK))])
        qt, k, vt4, h, kn = _inproj(x, mod, w_in[l].astype(BF16), gains, tab)

        k4 = k.reshape(b, s // KV_CHUNK, KV_CHUNK, KV_WIDTH)
        gain_t = jnp.broadcast_to(attn_out_g[l].reshape(N_KV_HEADS, GROUPS * HEAD_DIM, 1),
                                  (N_KV_HEADS, GROUPS * HEAD_DIM, Q_BLOCK))
        attn = _attention(qt, k4, vt4, kn.reshape(b, N_KV_HEADS, 1, s), gain_t)

        dw_w = jnp.zeros((CONV_WIDTH + 1, conv_ch), F32).at[:CONV_WIDTH].set(conv_dw_w[l])
        vecs = jnp.stack([conv_dw_b[l], conv_ln_g[l], conv_ln_b[l], conv_out_g[l].reshape(-1)])
        hc = _convmod(h, dw_w, vecs, w_conv_pw2[l].astype(BF16), gmat)

        wo_bf = w_o[l].astype(BF16)
        x1, u2 = _outproj(alpha, x, attn, hc, mod, wo_bf[:ATTN_WIDTH], wo_bf[ATTN_WIDTH:],
                          jnp.stack([ln1_g[l], ln1_b[l]]))

        dw4 = jnp.concatenate([ffn_dw_w[l], ffn_dw_b[l][None, :]], axis=0)
        x = _ffn(alpha, u2, x1, mod, w_up[l].astype(BF16), dw4, w_down[l].astype(BF16),
                 jnp.stack([ln2_g[l], ln2_b[l]]))
    return x
```

```python
import functools

import jax
import jax.numpy as jnp
from jax import lax
from jax.experimental import pallas as pl
from jax.experimental.pallas import tpu as pltpu

GRID_W = 64
HEAD_DIM = 64
N_HEADS = 8
N_KV_HEADS = 2
GROUPS = N_HEADS // N_KV_HEADS
ATTN_WIDTH = N_HEADS * HEAD_DIM
KV_WIDTH = N_KV_HEADS * HEAD_DIM
CONV_GROUP_DIM = 64
CONV_WIDTH = 31
CONV_PAD = (CONV_WIDTH - 1) // 2
FFN_CONV_WIDTH = 3
ROPE_THETA = 10000.0
ROPE_FREQS = HEAD_DIM // 4
EPS = 1e-6
Q_SCALE = HEAD_DIM ** -0.5 * 1.4426950408889634
SUM_ROWS = 16
V_ROWS = HEAD_DIM + SUM_ROWS

F32 = jnp.float32
BF16 = jnp.bfloat16

V7X_VMEM_LIMIT_BYTES = 56 * 1024 * 1024
ROW_BLOCK = 512
Q_BLOCK = 512
KV_CHUNK = ROW_BLOCK
HALO = 16
CONV_ROWS = 64
FF_CHUNK = 256
NORM_ROWS = 32
MAX_SLAB = 64
SCORES_AHEAD = 2
CHUNK_UNROLL = 4
MAX_SAFE_BOUND = 60.0
BOUND_SLACK = 1.0 + 2.0 ** -6


def _params(*sem):
    return pltpu.CompilerParams(dimension_semantics=sem, vmem_limit_bytes=V7X_VMEM_LIMIT_BYTES)


def _ln_noaffine(x):
    mu = jnp.mean(x, axis=-1, keepdims=True)
    xc = x - mu
    var = jnp.mean(xc * xc, axis=-1, keepdims=True)
    return xc * lax.rsqrt(var + EPS)


def _adaln_kernel(c_ref, w_ref, b_ref, o_ref):
    c = c_ref[...]
    c_act = c * jax.nn.sigmoid(c)
    w = w_ref[...]
    c_hi, w_hi = c_act.astype(BF16), w.astype(BF16)
    c_lo = (c_act - c_hi.astype(F32)).astype(BF16)
    w_lo = (w - w_hi.astype(F32)).astype(BF16)
    o_ref[...] = (jnp.dot(c_hi, w_hi, preferred_element_type=F32) +
                  jnp.dot(c_hi, w_lo, preferred_element_type=F32) +
                  jnp.dot(c_lo, w_hi, preferred_element_type=F32)) + b_ref[...]


def _adaln(c_pad, w, b):
    rows, d = c_pad.shape
    n = w.shape[1]
    return pl.pallas_call(
        _adaln_kernel,
        out_shape=jax.ShapeDtypeStruct((rows, n), F32),
        grid=(n // d,),
        in_specs=[pl.BlockSpec((rows, d), lambda j: (0, 0)),
                  pl.BlockSpec((d, d), lambda j: (0, j)),
                  pl.BlockSpec((1, d), lambda j: (0, j))],
        out_specs=pl.BlockSpec((rows, d), lambda j: (0, j)),
        compiler_params=_params("arbitrary"),
        name="adaln",
    )(c_pad, w, b)


def _rope_t(x, tab):
    f = ROPE_FREQS
    cr, sr, cc, sc = tab[0:f], tab[f:2 * f], tab[2 * f:3 * f], tab[3 * f:4 * f]
    x1r, x2r, x1c, x2c = x[0:f], x[f:2 * f], x[2 * f:3 * f], x[3 * f:4 * f]
    return jnp.concatenate([x1r * cr - x2r * sr, x2r * cr + x1r * sr,
                            x1c * cc - x2c * sc, x2c * cc + x1c * sc], axis=0)


def _norm_rope_t(xt, gain, tab):
    ms = jnp.mean(xt * xt, axis=0, keepdims=True)
    return _rope_t(xt * lax.rsqrt(ms + EPS) * gain, tab)


def _inproj_kernel(x_ref, mod_ref, w_ref, gq_ref, gk_ref, tab_ref, qt_ref, k_ref, vt_ref, h_ref, kn_ref,
                   u_ref, qkv_ref):
    shift, scale = mod_ref[0:1, :], mod_ref[1:2, :]
    for r in range(0, x_ref.shape[0], NORM_ROWS):
        u = _ln_noaffine(x_ref[r:r + NORM_ROWS, :]) * (1.0 + scale) + shift
        u_ref[r:r + NORM_ROWS, :] = u.astype(BF16)
    qkv_w = ATTN_WIDTH + 2 * KV_WIDTH
    u = u_ref[...]
    qkv_ref[...] = jnp.dot(u, w_ref[:, :qkv_w], preferred_element_type=F32)
    glu = jnp.dot(u, w_ref[:, qkv_w:], preferred_element_type=F32)

    qkv_t = qkv_ref[...].T
    tab = tab_ref[...]
    gq, gk = gq_ref[...], gk_ref[...]
    for h in range(N_HEADS):
        r = h * HEAD_DIM
        qh = _norm_rope_t(qkv_t[r:r + HEAD_DIM], gq, tab) * Q_SCALE
        qt_ref[r:r + HEAD_DIM, :] = qh.astype(BF16)
    k_rot = []
    for h in range(N_KV_HEADS):
        r = ATTN_WIDTH + h * HEAD_DIM
        kh = _norm_rope_t(qkv_t[r:r + HEAD_DIM], gk, tab).astype(BF16)
        k_rot.append(kh)
        khf = kh.astype(F32)
        kn_ref[h:h + 1, :] = jnp.sum(khf * khf, axis=0, keepdims=True)
    k_ref[...] = jnp.concatenate(k_rot, axis=0).astype(F32).T.astype(BF16)
    for h in range(N_KV_HEADS):
        r = ATTN_WIDTH + KV_WIDTH + h * HEAD_DIM
        vt_ref[h, 0:HEAD_DIM, :] = qkv_t[r:r + HEAD_DIM].astype(BF16)
        vt_ref[h, HEAD_DIM:, :] = jnp.ones((SUM_ROWS, vt_ref.shape[2]), BF16)

    conv_ch = glu.shape[1] // 2
    h_ref[...] = glu[:, :conv_ch] * jax.nn.sigmoid(glu[:, conv_ch:])


def _inproj(x, mod, w_bf, gq, gk, tab):
    b, s, d = x.shape
    t = ROW_BLOCK
    n_cols = w_bf.shape[1]
    conv_ch = (n_cols - ATTN_WIDTH - 2 * KV_WIDTH) // 2
    return pl.pallas_call(
        _inproj_kernel,
        out_shape=(jax.ShapeDtypeStruct((b, ATTN_WIDTH, s), BF16),
                   jax.ShapeDtypeStruct((b, s, KV_WIDTH), BF16),
                   jax.ShapeDtypeStruct((b, s // t, N_KV_HEADS, V_ROWS, t), BF16),
                   jax.ShapeDtypeStruct((b, s, conv_ch), F32),
                   jax.ShapeDtypeStruct((b, N_KV_HEADS, s), F32)),
        grid=(b, s // t),
        in_specs=[pl.BlockSpec((None, t, d), lambda bi, i: (bi, i, 0)),
                  pl.BlockSpec((None, 6, d), lambda bi, i: (bi, 0, 0)),
                  pl.BlockSpec((d, n_cols), lambda bi, i: (0, 0)),
                  pl.BlockSpec((HEAD_DIM, 1), lambda bi, i: (0, 0)),
                  pl.BlockSpec((HEAD_DIM, 1), lambda bi, i: (0, 0)),
                  pl.BlockSpec((HEAD_DIM, t), lambda bi, i: (0, i))],
        out_specs=(pl.BlockSpec((None, ATTN_WIDTH, t), lambda bi, i: (bi, 0, i)),
                   pl.BlockSpec((None, t, KV_WIDTH), lambda bi, i: (bi, i, 0)),
                   pl.BlockSpec((None, None, N_KV_HEADS, V_ROWS, t), lambda bi, i: (bi, i, 0, 0, 0)),
                   pl.BlockSpec((None, t, conv_ch), lambda bi, i: (bi, i, 0)),
                   pl.BlockSpec((None, N_KV_HEADS, t), lambda bi, i: (bi, 0, i))),
        scratch_shapes=[pltpu.VMEM((t, d), BF16),
                        pltpu.VMEM((t, ATTN_WIDTH + 2 * KV_WIDTH), F32)],
        compiler_params=_params("parallel", "parallel"),
        name="inproj",
    )(x, mod, w_bf, gq, gk, tab)


def _conv31_piece(ext_ref, dw_ref, vec_ref, z_ref, r):
    dw_b, ln_g, ln_b = vec_ref[0:1, :], vec_ref[1:2, :], vec_ref[2:3, :]
    base = HALO - CONV_PAD
    sub = 8
    acc = jnp.broadcast_to(dw_b, (CONV_ROWS, dw_b.shape[1]))
    for s in range(sub):
        part = None
        for j in range(CONV_WIDTH):
            if (j + base) % sub != s:
                continue
            lo = pl.multiple_of(r + ((j + base) - s), sub)
            term = dw_ref[j:j + 1, :] * ext_ref[pl.ds(lo, CONV_ROWS + sub), :]
            part = term if part is None else part + term
        acc = acc + part[s:s + CONV_ROWS, :]
    y = _ln_noaffine(acc) * ln_g + ln_b
    z_ref[pl.ds(pl.multiple_of(r, CONV_ROWS), CONV_ROWS), :] = (y * jax.nn.sigmoid(y)).astype(BF16)


def _attn_kernel(qt_ref, k_ref, vt_ref, kn_ref, g_ref, hprev_ref, hcur_ref, hnext_ref, dw_ref, vec_ref,
                 pw_ref, gmat_ref, o_ref, hc_ref, qp_ref, s_ref, p_ref, m_ref, acc_ref, ext_ref, z_ref):
    grp = pl.program_id(1)
    n_chunks, tk = k_ref.shape[0], k_ref.shape[1]
    tq = qt_ref.shape[1]
    unroll = min(CHUNK_UNROLL, n_chunks)
    n_trips = n_chunks // unroll

    ct = hcur_ref.shape[0]
    pieces = ct // CONV_ROWS // n_trips

    def conv_pieces(tr):
        for i in range(pieces):
            _conv31_piece(ext_ref, dw_ref, vec_ref, z_ref, (tr * pieces + i) * CONV_ROWS)

    conv_blk = pl.program_id(2) * N_KV_HEADS + grp
    conv_last = pl.num_programs(2) * N_KV_HEADS - 1
    ext_ref[0:HALO, :] = jnp.where(conv_blk > 0, hprev_ref[...], 0.0)
    ext_ref[HALO:HALO + ct, :] = hcur_ref[...]
    ext_ref[HALO + ct:, :] = jnp.where(conv_blk < conv_last, hnext_ref[...], 0.0)

    first = grp == 0
    k_norm2_max = jnp.max(kn_ref[...], axis=1, keepdims=True)
    bounds = []
    for h in range(GROUPS):
        qh = qt_ref[h * HEAD_DIM:(h + 1) * HEAD_DIM, :]
        z = jnp.zeros_like(qh)
        qp_ref[h, 0:HEAD_DIM, :] = jnp.where(first, qh, z)
        qp_ref[h, HEAD_DIM:2 * HEAD_DIM, :] = jnp.where(first, z, qh)
        qf = qh.astype(F32)
        q_norm2 = jnp.sum(qf * qf, axis=0, keepdims=True)
        bounds.append(jnp.sqrt(q_norm2 * k_norm2_max) * BOUND_SLACK)
    acc_ref[...] = jnp.zeros(acc_ref.shape, F32)
    bound_max = jnp.max(jnp.concatenate(bounds, axis=0))

    def scores(c, h):
        return jnp.dot(k_ref[c], qp_ref[h], preferred_element_type=F32)

    @pl.when(bound_max <= MAX_SAFE_BOUND)
    def _():
        for h in range(GROUPS):
            m_ref[h] = bounds[h]

        def probs(c, h, slot):
            p_ref[slot] = jnp.exp2(scores(c, h) - m_ref[h]).astype(BF16)

        probs(0, 0, 0)

        def trip(tr, carry):
            conv_pieces(tr)
            for cu in range(unroll):
                c = tr * unroll + cu
                vc = vt_ref[c]
                c_next = jnp.minimum(c + 1, n_chunks - 1)
                for h in range(GROUPS):
                    if h + 1 < GROUPS:
                        probs(c, h + 1, (h + 1) % 2)
                    else:
                        probs(c_next, 0, 0)
                    acc_ref[h] += jnp.dot(vc, p_ref[h % 2], preferred_element_type=F32)
            return carry

        lax.fori_loop(0, n_trips, trip, 0)

    @pl.when(jnp.logical_not(bound_max <= MAX_SAFE_BOUND))
    def _():
        m_ref[...] = jnp.full(m_ref.shape, -jnp.inf, F32)
        for h in range(SCORES_AHEAD):
            s_ref[h] = scores(0, h)

        def trip(tr, carry):
            conv_pieces(tr)
            for cu in range(unroll):
                c = tr * unroll + cu
                vc = vt_ref[c]
                c_next = jnp.minimum(c + 1, n_chunks - 1)
                for h in range(GROUPS):
                    ahead = h + SCORES_AHEAD
                    if ahead < GROUPS:
                        s_ref[ahead] = scores(c, ahead)
                    else:
                        s_ref[ahead - GROUPS] = scores(c_next, ahead - GROUPS)
                    s = s_ref[h]
                    m_old = m_ref[h]
                    slab_max = jnp.max(s.reshape(tk // MAX_SLAB, MAX_SLAB, tq), axis=0)
                    m_new = jnp.maximum(m_old, jnp.max(slab_max, axis=0, keepdims=True))
                    p = jnp.exp2(s - m_new)
                    alpha = jnp.exp2(m_old - m_new)
                    pv = jnp.dot(vc, p.astype(BF16), preferred_element_type=F32)
                    acc_ref[h] = alpha * acc_ref[h] + pv
                    m_ref[h] = m_new
            return carry

        lax.fori_loop(0, n_trips, trip, 0)

    pw = jnp.dot(z_ref[...], pw_ref[...], preferred_element_type=F32)
    pw_ms = _group_mean(pw * pw, gmat_ref[...])
    hc_ref[...] = (pw * lax.rsqrt(pw_ms + EPS) * vec_ref[3:4, :]).astype(BF16)

    outs = []
    for h in range(GROUPS):
        o = acc_ref[h, 0:HEAD_DIM, :] / acc_ref[h, HEAD_DIM:HEAD_DIM + 1, :]
        ms = jnp.mean(o * o, axis=0, keepdims=True)
        outs.append(o * lax.rsqrt(ms + EPS) * g_ref[h * HEAD_DIM:(h + 1) * HEAD_DIM, :])
    o_ref[...] = jnp.concatenate(outs, axis=0).T.astype(BF16)


def _attention_and_conv(qt, k4, vt4, kn4, gain_t, h, dw_w, vecs, pw_bf, gmat):
    b, _, s = qt.shape
    n_chunks, tk = k4.shape[1], k4.shape[2]
    tq = Q_BLOCK
    gw = GROUPS * HEAD_DIM
    ch = h.shape[2]
    ct = tq // N_KV_HEADS
    n_trips = n_chunks // min(CHUNK_UNROLL, n_chunks)
    assert n_chunks % min(CHUNK_UNROLL, n_chunks) == 0 and ct % (CONV_ROWS * n_trips) == 0
    hb = ct // HALO
    n_halo = s // HALO

    def conv_blk(g, qi):
        return qi * N_KV_HEADS + g

    return pl.pallas_call(
        _attn_kernel,
        out_shape=(jax.ShapeDtypeStruct((b, s, ATTN_WIDTH), BF16),
                   jax.ShapeDtypeStruct((b, s, ch), BF16)),
        grid=(b, N_KV_HEADS, s // tq),
        in_specs=[pl.BlockSpec((None, gw, tq), lambda bi, g, qi: (bi, g, qi)),
                  pl.BlockSpec((None, n_chunks, tk, KV_WIDTH), lambda bi, g, qi: (bi, 0, 0, 0)),
                  pl.BlockSpec((None, n_chunks, None, V_ROWS, tk), lambda bi, g, qi: (bi, 0, g, 0, 0)),
                  pl.BlockSpec((None, None, 1, s), lambda bi, g, qi: (bi, g, 0, 0)),
                  pl.BlockSpec((None, gw, 1), lambda bi, g, qi: (g, 0, 0)),
                  pl.BlockSpec((None, HALO, ch),
                               lambda bi, g, qi: (bi, jnp.maximum(conv_blk(g, qi) * hb - 1, 0), 0)),
                  pl.BlockSpec((None, ct, ch), lambda bi, g, qi: (bi, conv_blk(g, qi), 0)),
                  pl.BlockSpec((None, HALO, ch),
                               lambda bi, g, qi: (bi, jnp.minimum((conv_blk(g, qi) + 1) * hb, n_halo - 1), 0)),
                  pl.BlockSpec(dw_w.shape, lambda bi, g, qi: (0, 0)),
                  pl.BlockSpec(vecs.shape, lambda bi, g, qi: (0, 0)),
                  pl.BlockSpec(pw_bf.shape, lambda bi, g, qi: (0, 0)),
                  pl.BlockSpec(gmat.shape, lambda bi, g, qi: (0, 0))],
        out_specs=(pl.BlockSpec((None, tq, gw), lambda bi, g, qi: (bi, qi, g)),
                   pl.BlockSpec((None, ct, ch), lambda bi, g, qi: (bi, conv_blk(g, qi), 0))),
        scratch_shapes=[pltpu.VMEM((GROUPS, 2 * HEAD_DIM, tq), BF16),
                        pltpu.VMEM((GROUPS, tk, tq), F32),
                        pltpu.VMEM((2, tk, tq), BF16),
                        pltpu.VMEM((GROUPS, 1, tq), F32),
                        pltpu.VMEM((GROUPS, V_ROWS, tq), F32),
                        pltpu.VMEM((ct + 2 * HALO, ch), F32),
                        pltpu.VMEM((ct, ch), BF16)],
        compiler_params=_params("parallel", "parallel", "parallel"),
        name="attn",
    )(qt, k4, vt4, kn4, gain_t, h, h, h, dw_w, vecs, pw_bf, gmat)


def _group_mean(sq, gmat):
    hi = sq.astype(BF16)
    lo = (sq - hi.astype(F32)).astype(BF16)
    return (jnp.dot(hi, gmat, preferred_element_type=F32) +
            jnp.dot(lo, gmat, preferred_element_type=F32))


def _outproj_kernel(alpha, x_ref, a_ref, c_ref, mod_ref, wo_ref, ln_ref, x1_ref, u2_ref, mix_ref):
    t = x_ref.shape[0]
    half = t // 2
    aw = a_ref.shape[1]
    for r in (0, half):
        mix_ref[r:r + half, :] = (
            jnp.dot(a_ref[r:r + half, :], wo_ref[0:aw, :], preferred_element_type=F32) +
            jnp.dot(c_ref[r:r + half, :], wo_ref[aw:, :], preferred_element_type=F32))
    gate1, shift2, scale2 = mod_ref[2:3, :], mod_ref[3:4, :], mod_ref[4:5, :]
    for r in range(0, t, NORM_ROWS):
        y = alpha * x_ref[r:r + NORM_ROWS, :] + gate1 * mix_ref[r:r + NORM_ROWS, :]
        x1 = _ln_noaffine(y) * ln_ref[0:1, :] + ln_ref[1:2, :]
        x1_ref[r:r + NORM_ROWS, :] = x1
        u2_ref[r:r + NORM_ROWS, :] = (_ln_noaffine(x1) * (1.0 + scale2) + shift2).astype(BF16)


def _outproj(alpha, x, attn, hc, mod, wo_bf, ln1):
    b, s, d = x.shape
    t = ROW_BLOCK
    aw, cw = attn.shape[2], hc.shape[2]
    return pl.pallas_call(
        functools.partial(_outproj_kernel, alpha),
        out_shape=(jax.ShapeDtypeStruct((b, s, d), F32),
                   jax.ShapeDtypeStruct((b, s, d), BF16)),
        grid=(b, s // t),
        in_specs=[pl.BlockSpec((None, t, d), lambda bi, i: (bi, i, 0)),
                  pl.BlockSpec((None, t, aw), lambda bi, i: (bi, i, 0)),
                  pl.BlockSpec((None, t, cw), lambda bi, i: (bi, i, 0)),
                  pl.BlockSpec((None, 6, d), lambda bi, i: (bi, 0, 0)),
                  pl.BlockSpec((aw + cw, d), lambda bi, i: (0, 0)),
                  pl.BlockSpec((2, d), lambda bi, i: (0, 0))],
        out_specs=(pl.BlockSpec((None, t, d), lambda bi, i: (bi, i, 0)),
                   pl.BlockSpec((None, t, d), lambda bi, i: (bi, i, 0))),
        scratch_shapes=[pltpu.VMEM((t, d), F32)],
        compiler_params=_params("parallel", "parallel"),
        name="outproj",
    )(x, attn, hc, mod, wo_bf, ln1)


def _ffn_kernel(alpha, prev_ref, cur_ref, next_ref, x1_ref, mod_ref, wup_ref, dw_ref, wdn_ref, ln_ref,
                o_ref, ext_ref, hv_ref, hg_ref, acc_ref):
    i = pl.program_id(1)
    last = pl.num_programs(1) - 1
    t = cur_ref.shape[0]
    d_ff = wdn_ref.shape[0]
    ext_ref[0:HALO, :] = jnp.where(i > 0, prev_ref[...], jnp.zeros_like(prev_ref[...]))
    ext_ref[HALO:HALO + t, :] = cur_ref[...]
    ext_ref[HALO + t:, :] = jnp.where(i < last, next_ref[...], jnp.zeros_like(next_ref[...]))
    ext = ext_ref[...]

    half = t // 2

    def conv3(h_ref, slot, col, r0):
        w = dw_ref[:, col:col + FF_CHUNK]
        lo = HALO + r0
        return (w[0:1] * h_ref[slot, lo - 1:lo - 1 + half, :] + w[1:2] * h_ref[slot, lo:lo + half, :] +
                w[2:3] * h_ref[slot, lo + 1:lo + 1 + half, :] + w[3:4])

    def up_val(c, slot):
        hv_ref[slot] = jnp.dot(ext, wup_ref[:, c:c + FF_CHUNK], preferred_element_type=F32)

    def up_gate(c, slot):
        hg_ref[slot] = jnp.dot(ext, wup_ref[:, d_ff + c:d_ff + c + FF_CHUNK], preferred_element_type=F32)

    up_val(0, 0)
    up_gate(0, 0)
    acc_ref[...] = jnp.zeros(acc_ref.shape, F32)
    for n, c in enumerate(range(0, d_ff, FF_CHUNK)):
        slot = n % 2
        for hi, up_next in enumerate((up_val, up_gate)):
            if c + FF_CHUNK < d_ff:
                up_next(c + FF_CHUNK, 1 - slot)
            r0 = hi * half
            val = conv3(hv_ref, slot, c, r0)
            gt = conv3(hg_ref, slot, d_ff + c, r0)
            act = 0.5 * gt * (1.0 + lax.erf(gt * (2.0 ** -0.5))) * val
            acc_ref[r0:r0 + half, :] += jnp.dot(act.astype(BF16), wdn_ref[c:c + FF_CHUNK, :],
                                                preferred_element_type=F32)

    gate2 = mod_ref[5:6, :]
    for r in range(0, t, NORM_ROWS):
        y = alpha * x1_ref[r:r + NORM_ROWS, :] + gate2 * acc_ref[r:r + NORM_ROWS, :]
        o_ref[r:r + NORM_ROWS, :] = _ln_noaffine(y) * ln_ref[0:1, :] + ln_ref[1:2, :]


def _ffn(alpha, u2, x1, mod, wup_bf, dw4, wdn_bf, ln2):
    b, s, d = x1.shape
    t = ROW_BLOCK
    hb = t // HALO
    n_halo = s // HALO
    resident = dict(pipeline_mode=pl.Buffered(1))
    return pl.pallas_call(
        functools.partial(_ffn_kernel, alpha),
        out_shape=jax.ShapeDtypeStruct((b, s, d), F32),
        grid=(b, s // t),
        in_specs=[pl.BlockSpec((None, HALO, d), lambda bi, i: (bi, jnp.maximum(i * hb - 1, 0), 0)),
                  pl.BlockSpec((None, t, d), lambda bi, i: (bi, i, 0)),
                  pl.BlockSpec((None, HALO, d), lambda bi, i: (bi, jnp.minimum((i + 1) * hb, n_halo - 1), 0)),
                  pl.BlockSpec((None, t, d), lambda bi, i: (bi, i, 0)),
                  pl.BlockSpec((None, 6, d), lambda bi, i: (bi, 0, 0)),
                  pl.BlockSpec(wup_bf.shape, lambda bi, i: (0, 0), **resident),
                  pl.BlockSpec(dw4.shape, lambda bi, i: (0, 0)),
                  pl.BlockSpec(wdn_bf.shape, lambda bi, i: (0, 0), **resident),
                  pl.BlockSpec((2, d), lambda bi, i: (0, 0))],
        out_specs=pl.BlockSpec((None, t, d), lambda bi, i: (bi, i, 0)),
        scratch_shapes=[pltpu.VMEM((t + 2 * HALO, d), BF16),
                        pltpu.VMEM((2, t + 2 * HALO, FF_CHUNK), F32),
                        pltpu.VMEM((2, t + 2 * HALO, FF_CHUNK), F32),
                        pltpu.VMEM((t, d), F32)],
        compiler_params=_params("parallel", "parallel"),
        name="ffn",
    )(u2, u2, u2, x1, mod, wup_bf, dw4, wdn_bf, ln2)


def _rope_table_t(seq_len):
    pos = jnp.arange(seq_len, dtype=jnp.int32)
    rows = (pos // GRID_W).astype(F32)
    cols = (pos % GRID_W).astype(F32)
    inv_freq = ROPE_THETA ** (-jnp.arange(ROPE_FREQS, dtype=F32) / ROPE_FREQS)
    ang_r = inv_freq[:, None] * rows[None, :]
    ang_c = inv_freq[:, None] * cols[None, :]
    return jnp.concatenate([jnp.cos(ang_r), jnp.sin(ang_r), jnp.cos(ang_c), jnp.sin(ang_c)], axis=0)


def kernel(x, c, w_ada, b_ada, w_in, q_norm_g, k_norm_g, conv_dw_w, conv_dw_b, conv_ln_g, conv_ln_b,
           w_conv_pw2, attn_out_g, conv_out_g, w_o, ln1_g, ln1_b, w_up, ffn_dw_w, ffn_dw_b, w_down,
           ln2_g, ln2_b):
    b, s, d = x.shape
    depth = w_ada.shape[0]
    alpha = (2.0 * depth) ** 0.25
    conv_ch = w_conv_pw2.shape[1]
    assert s % ROW_BLOCK == 0 and s % Q_BLOCK == 0 and s % GRID_W == 0
    assert w_down.shape[1] % FF_CHUNK == 0 and b <= 8

    tab = _rope_table_t(s)
    c_pad = jnp.pad(c, ((0, 8 - b), (0, 0)))
    gidx = jnp.arange(conv_ch) // CONV_GROUP_DIM
    gmat = jnp.where(gidx[:, None] == gidx[None, :], 1.0 / CONV_GROUP_DIM, 0.0).astype(BF16)

    for l in range(depth):
        mod = _adaln(c_pad, w_ada[l], b_ada[l][None, :])[:b].reshape(b, 6, d)

        qt, k, vt4, h, kn = _inproj(x, mod, w_in[l].astype(BF16), q_norm_g[l].reshape(HEAD_DIM, 1),
                                    k_norm_g[l].reshape(HEAD_DIM, 1), tab)

        k4 = k.reshape(b, s // KV_CHUNK, KV_CHUNK, KV_WIDTH)
        gain_t = attn_out_g[l].reshape(N_KV_HEADS, GROUPS * HEAD_DIM, 1)
        vecs = jnp.stack([conv_dw_b[l], conv_ln_g[l], conv_ln_b[l], conv_out_g[l].reshape(-1)])
        attn, hc = _attention_and_conv(qt, k4, vt4, kn.reshape(b, N_KV_HEADS, 1, s), gain_t,
                                       h, conv_dw_w[l], vecs, w_conv_pw2[l].astype(BF16), gmat)

        x1, u2 = _outproj(alpha, x, attn, hc, mod, w_o[l].astype(BF16), jnp.stack([ln1_g[l], ln1_b[l]]))

        dw4 = jnp.concatenate([ffn_dw_w[l], ffn_dw_b[l][None, :]], axis=0)
        x = _ffn(alpha, u2, x1, mod, w_up[l].astype(BF16), dw4, w_down[l].astype(BF16),
                 jnp.stack([ln2_g[l], ln2_b[l]]))
    return x
```

```python
import functools

import jax
import jax.numpy as jnp
from jax import lax
from jax.experimental import pallas as pl
from jax.experimental.pallas import tpu as pltpu

GRID_W = 64
HEAD_DIM = 64
N_HEADS = 8
N_KV_HEADS = 2
GROUPS = N_HEADS // N_KV_HEADS
ATTN_WIDTH = N_HEADS * HEAD_DIM
KV_WIDTH = N_KV_HEADS * HEAD_DIM
CONV_GROUP_DIM = 64
CONV_WIDTH = 31
CONV_PAD = (CONV_WIDTH - 1) // 2
FFN_CONV_WIDTH = 3
ROPE_THETA = 10000.0
ROPE_FREQS = HEAD_DIM // 4
EPS = 1e-6
Q_SCALE = HEAD_DIM ** -0.5 * 1.4426950408889634
SUM_ROWS = 16
V_ROWS = HEAD_DIM + SUM_ROWS

F32 = jnp.float32
BF16 = jnp.bfloat16

V7X_VMEM_LIMIT_BYTES = 56 * 1024 * 1024
ROW_BLOCK = 512
Q_BLOCK = 512
KV_CHUNK = ROW_BLOCK
HALO = 16
CONV_ROWS = 64
FF_CHUNK = 256
FFN_PIECES = 4
NORM_ROWS = 32
MAX_SLAB = 64
SCORES_AHEAD = 2
CHUNK_UNROLL = 4
MAX_SAFE_BOUND = 60.0
BOUND_SLACK = 1.0 + 2.0 ** -6


def _params(*sem):
    return pltpu.CompilerParams(dimension_semantics=sem, vmem_limit_bytes=V7X_VMEM_LIMIT_BYTES)


def _ln_noaffine(x):
    mu = jnp.mean(x, axis=-1, keepdims=True)
    xc = x - mu
    var = jnp.mean(xc * xc, axis=-1, keepdims=True)
    return xc * lax.rsqrt(var + EPS)


def _adaln_kernel(c_ref, w_ref, b_ref, o_ref):
    c = c_ref[...]
    c_act = c * jax.nn.sigmoid(c)
    w = w_ref[...]
    c_hi, w_hi = c_act.astype(BF16), w.astype(BF16)
    c_lo = (c_act - c_hi.astype(F32)).astype(BF16)
    w_lo = (w - w_hi.astype(F32)).astype(BF16)
    o_ref[...] = (jnp.dot(c_hi, w_hi, preferred_element_type=F32) +
                  jnp.dot(c_hi, w_lo, preferred_element_type=F32) +
                  jnp.dot(c_lo, w_hi, preferred_element_type=F32)) + b_ref[...]


def _adaln(c_pad, w, b):
    rows, d = c_pad.shape
    n = w.shape[1]
    return pl.pallas_call(
        _adaln_kernel,
        out_shape=jax.ShapeDtypeStruct((rows, n), F32),
        grid=(n // d,),
        in_specs=[pl.BlockSpec((rows, d), lambda j: (0, 0)),
                  pl.BlockSpec((d, d), lambda j: (0, j)),
                  pl.BlockSpec((1, d), lambda j: (0, j))],
        out_specs=pl.BlockSpec((rows, d), lambda j: (0, j)),
        compiler_params=_params("arbitrary"),
        name="adaln",
    )(c_pad, w, b)


def _rope_t(x, tab):
    f = ROPE_FREQS
    cr, sr, cc, sc = tab[0:f], tab[f:2 * f], tab[2 * f:3 * f], tab[3 * f:4 * f]
    x1r, x2r, x1c, x2c = x[0:f], x[f:2 * f], x[2 * f:3 * f], x[3 * f:4 * f]
    return jnp.concatenate([x1r * cr - x2r * sr, x2r * cr + x1r * sr,
                            x1c * cc - x2c * sc, x2c * cc + x1c * sc], axis=0)


def _norm_rope_t(xt, gain, tab):
    ms = jnp.mean(xt * xt, axis=0, keepdims=True)
    return _rope_t(xt * lax.rsqrt(ms + EPS) * gain, tab)


def _inproj_kernel(x_ref, mod_ref, w_ref, gains_ref, tab_ref, qt_ref, k_ref, vt_ref, h_ref, kn_ref,
                   u_ref, proj_ref):
    shift, scale = mod_ref[0:1, :], mod_ref[1:2, :]
    t = x_ref.shape[0]
    half = t // 2
    qkv_w = ATTN_WIDTH + 2 * KV_WIDTH
    conv_ch = (w_ref.shape[1] - qkv_w) // 2

    for r0 in (0, half):
        for r in range(r0, r0 + half, NORM_ROWS):
            u = _ln_noaffine(x_ref[r:r + NORM_ROWS, :]) * (1.0 + scale) + shift
            u_ref[r:r + NORM_ROWS, :] = u.astype(BF16)
        proj_ref[r0:r0 + half, :] = jnp.dot(u_ref[r0:r0 + half, :], w_ref[...],
                                            preferred_element_type=F32)

    for r0 in (0, half):
        cols = slice(r0, r0 + half)
        qkv_t = proj_ref[r0:r0 + half, 0:qkv_w].T
        tab = tab_ref[:, cols]
        gq, gk = gains_ref[0, :, 0:half], gains_ref[1, :, 0:half]
        for h in range(N_HEADS):
            r = h * HEAD_DIM
            qh = _norm_rope_t(qkv_t[r:r + HEAD_DIM], gq, tab) * Q_SCALE
            qt_ref[r:r + HEAD_DIM, cols] = qh.astype(BF16)
        k_rot = []
        for h in range(N_KV_HEADS):
            r = ATTN_WIDTH + h * HEAD_DIM
            kh = _norm_rope_t(qkv_t[r:r + HEAD_DIM], gk, tab).astype(BF16)
            k_rot.append(kh)
            khf = kh.astype(F32)
            kn_ref[h:h + 1, cols] = jnp.sum(khf * khf, axis=0, keepdims=True)
        k_ref[cols, :] = jnp.concatenate(k_rot, axis=0).astype(F32).T.astype(BF16)
        for h in range(N_KV_HEADS):
            r = ATTN_WIDTH + KV_WIDTH + h * HEAD_DIM
            vt_ref[h, 0:HEAD_DIM, cols] = qkv_t[r:r + HEAD_DIM].astype(BF16)
            vt_ref[h, HEAD_DIM:, cols] = jnp.ones((SUM_ROWS, half), BF16)

        a = proj_ref[r0:r0 + half, qkv_w:qkv_w + conv_ch]
        g = proj_ref[r0:r0 + half, qkv_w + conv_ch:]
        h_ref[r0:r0 + half, :] = a * jax.nn.sigmoid(g)


def _inproj(x, mod, w_bf, gains, tab):
    b, s, d = x.shape
    t = ROW_BLOCK
    n_cols = w_bf.shape[1]
    conv_ch = (n_cols - ATTN_WIDTH - 2 * KV_WIDTH) // 2
    return pl.pallas_call(
        _inproj_kernel,
        out_shape=(jax.ShapeDtypeStruct((b, ATTN_WIDTH, s), BF16),
                   jax.ShapeDtypeStruct((b, s, KV_WIDTH), BF16),
                   jax.ShapeDtypeStruct((b, s // t, N_KV_HEADS, V_ROWS, t), BF16),
                   jax.ShapeDtypeStruct((b, s, conv_ch), F32),
                   jax.ShapeDtypeStruct((b, N_KV_HEADS, s), F32)),
        grid=(b, s // t),
        in_specs=[pl.BlockSpec((None, t, d), lambda bi, i: (bi, i, 0)),
                  pl.BlockSpec((None, 6, d), lambda bi, i: (bi, 0, 0)),
                  pl.BlockSpec((d, n_cols), lambda bi, i: (0, 0)),
                  pl.BlockSpec((2, HEAD_DIM, t), lambda bi, i: (0, 0, 0)),
                  pl.BlockSpec((HEAD_DIM, t), lambda bi, i: (0, i))],
        out_specs=(pl.BlockSpec((None, ATTN_WIDTH, t), lambda bi, i: (bi, 0, i)),
                   pl.BlockSpec((None, t, KV_WIDTH), lambda bi, i: (bi, i, 0)),
                   pl.BlockSpec((None, None, N_KV_HEADS, V_ROWS, t), lambda bi, i: (bi, i, 0, 0, 0)),
                   pl.BlockSpec((None, t, conv_ch), lambda bi, i: (bi, i, 0)),
                   pl.BlockSpec((None, N_KV_HEADS, t), lambda bi, i: (bi, 0, i))),
        scratch_shapes=[pltpu.VMEM((t, d), BF16),
                        pltpu.VMEM((t, n_cols), F32)],
        compiler_params=_params("parallel", "parallel"),
        name="inproj",
    )(x, mod, w_bf, gains, tab)


def _attn_kernel(qt_ref, k_ref, vt_ref, kn_ref, g_ref, o_ref, qp_ref, s_ref, p_ref, m_ref, l_ref, acc_ref):
    grp = pl.program_id(1)
    n_chunks, tk = k_ref.shape[0], k_ref.shape[1]
    tq = qt_ref.shape[1]

    first = grp == 0
    k_norm2_max = jnp.max(kn_ref[...], axis=1, keepdims=True)
    bounds = []
    for h in range(GROUPS):
        qh = qt_ref[h * HEAD_DIM:(h + 1) * HEAD_DIM, :]
        z = jnp.zeros_like(qh)
        qp_ref[h, 0:HEAD_DIM, :] = jnp.where(first, qh, z)
        qp_ref[h, HEAD_DIM:2 * HEAD_DIM, :] = jnp.where(first, z, qh)
        qf = qh.astype(F32)
        q_norm2 = jnp.sum(qf * qf, axis=0, keepdims=True)
        bounds.append(jnp.sqrt(q_norm2 * k_norm2_max) * BOUND_SLACK)
    acc_ref[...] = jnp.zeros(acc_ref.shape, F32)
    bound_max = jnp.max(jnp.concatenate(bounds, axis=0))

    def scores(c, h):
        return jnp.dot(k_ref[c], qp_ref[h], preferred_element_type=F32)

    @pl.when(bound_max <= MAX_SAFE_BOUND)
    def _():
        for h in range(GROUPS):
            m_ref[h] = bounds[h]
        l_ref[...] = jnp.zeros(l_ref.shape, F32)

        def probs(c, h, slot, weight=None):
            p = jnp.exp2(scores(c, h) - m_ref[h])
            p_ref[slot] = p.astype(BF16)
            psum = jnp.sum(p.reshape(tk // MAX_SLAB, MAX_SLAB, tq), axis=0)
            psum = jnp.sum(psum.reshape(MAX_SLAB // 8, 8, tq), axis=0)
            l_ref[h] += psum if weight is None else psum * weight

        probs(0, 0, 0)

        def chunk(c, carry):
            vc = vt_ref[c, 0:HEAD_DIM, :]
            c_next = jnp.minimum(c + 1, n_chunks - 1)
            real_next = (c + 1 < n_chunks).astype(F32)
            for h in range(GROUPS):
                if h + 1 < GROUPS:
                    probs(c, h + 1, (h + 1) % 2)
                else:
                    probs(c_next, 0, 0, real_next)
                acc_ref[h, 0:HEAD_DIM, :] += jnp.dot(vc, p_ref[h % 2], preferred_element_type=F32)
            return carry

        lax.fori_loop(0, n_chunks, chunk, 0, unroll=CHUNK_UNROLL)
        for h in range(GROUPS):
            acc_ref[h, HEAD_DIM:HEAD_DIM + 1, :] = jnp.sum(l_ref[h], axis=0, keepdims=True)

    @pl.when(jnp.logical_not(bound_max <= MAX_SAFE_BOUND))
    def _():
        m_ref[...] = jnp.full(m_ref.shape, -jnp.inf, F32)
        for h in range(SCORES_AHEAD):
            s_ref[h] = scores(0, h)

        def chunk(c, carry):
            vc = vt_ref[c]
            c_next = jnp.minimum(c + 1, n_chunks - 1)
            for h in range(GROUPS):
                ahead = h + SCORES_AHEAD
                if ahead < GROUPS:
                    s_ref[ahead] = scores(c, ahead)
                else:
                    s_ref[ahead - GROUPS] = scores(c_next, ahead - GROUPS)
                s = s_ref[h]
                m_old = m_ref[h]
                slab_max = jnp.max(s.reshape(tk // MAX_SLAB, MAX_SLAB, tq), axis=0)
                m_new = jnp.maximum(m_old, jnp.max(slab_max, axis=0, keepdims=True))
                p = jnp.exp2(s - m_new)
                alpha = jnp.exp2(m_old - m_new)
                pv = jnp.dot(vc, p.astype(BF16), preferred_element_type=F32)
                acc_ref[h] = alpha * acc_ref[h] + pv
                m_ref[h] = m_new
            return carry

        lax.fori_loop(0, n_chunks, chunk, 0, unroll=2)

    outs = []
    for h in range(GROUPS):
        o = acc_ref[h, 0:HEAD_DIM, :] / acc_ref[h, HEAD_DIM:HEAD_DIM + 1, :]
        ms = jnp.mean(o * o, axis=0, keepdims=True)
        outs.append(o * lax.rsqrt(ms + EPS) * g_ref[h * HEAD_DIM:(h + 1) * HEAD_DIM, :])
    o_ref[...] = jnp.concatenate(outs, axis=0).T.astype(BF16)


def _attention(qt, k4, vt4, kn4, gain_t):
    b, _, s = qt.shape
    n_chunks, tk = k4.shape[1], k4.shape[2]
    tq = Q_BLOCK
    gw = GROUPS * HEAD_DIM
    return pl.pallas_call(
        _attn_kernel,
        out_shape=jax.ShapeDtypeStruct((b, s, ATTN_WIDTH), BF16),
        grid=(b, N_KV_HEADS, s // tq),
        in_specs=[pl.BlockSpec((None, gw, tq), lambda bi, g, qi: (bi, g, qi)),
                  pl.BlockSpec((None, n_chunks, tk, KV_WIDTH), lambda bi, g, qi: (bi, 0, 0, 0)),
                  pl.BlockSpec((None, n_chunks, None, V_ROWS, tk), lambda bi, g, qi: (bi, 0, g, 0, 0)),
                  pl.BlockSpec((None, None, 1, s), lambda bi, g, qi: (bi, g, 0, 0)),
                  pl.BlockSpec((None, gw, tq), lambda bi, g, qi: (g, 0, 0))],
        out_specs=pl.BlockSpec((None, tq, gw), lambda bi, g, qi: (bi, qi, g)),
        scratch_shapes=[pltpu.VMEM((GROUPS, 2 * HEAD_DIM, tq), BF16),
                        pltpu.VMEM((GROUPS, tk, tq), F32),
                        pltpu.VMEM((2, tk, tq), BF16),
                        pltpu.VMEM((GROUPS, 1, tq), F32),
                        pltpu.VMEM((GROUPS, 8, tq), F32),
                        pltpu.VMEM((GROUPS, V_ROWS, tq), F32)],
        compiler_params=_params("parallel", "parallel", "parallel"),
        name="attn",
    )(qt, k4, vt4, kn4, gain_t)


def _group_mean(sq, gmat):
    hi = sq.astype(BF16)
    lo = (sq - hi.astype(F32)).astype(BF16)
    return (jnp.dot(hi, gmat, preferred_element_type=F32) +
            jnp.dot(lo, gmat, preferred_element_type=F32))


def _convmod_kernel(prev_ref, cur_ref, next_ref, dw_ref, vec_ref, pw_ref, gmat_ref, o_ref, ext_ref, z_ref):
    i = pl.program_id(1)
    last = pl.num_programs(1) - 1
    t = cur_ref.shape[0]
    ext_ref[0:HALO, :] = jnp.where(i > 0, prev_ref[...], 0.0)
    ext_ref[HALO:HALO + t, :] = cur_ref[...]
    ext_ref[HALO + t:, :] = jnp.where(i < last, next_ref[...], 0.0)

    dw_b, ln_g, ln_b, out_g = vec_ref[0:1, :], vec_ref[1:2, :], vec_ref[2:3, :], vec_ref[3:4, :]
    base = HALO - CONV_PAD
    sub = 8
    for r in range(0, t, CONV_ROWS):
        acc = jnp.broadcast_to(dw_b, (CONV_ROWS, dw_b.shape[1]))
        for s in range(sub):
            part = None
            for j in range(CONV_WIDTH):
                if (j + base) % sub != s:
                    continue
                lo = r + (j + base) - s
                term = dw_ref[j:j + 1, :] * ext_ref[lo:lo + CONV_ROWS + sub, :]
                part = term if part is None else part + term
            acc = acc + part[s:s + CONV_ROWS, :]
        y = _ln_noaffine(acc) * ln_g + ln_b
        z_ref[r:r + CONV_ROWS, :] = (y * jax.nn.sigmoid(y)).astype(BF16)

    pw = jnp.dot(z_ref[...], pw_ref[...], preferred_element_type=F32)
    ms = _group_mean(pw * pw, gmat_ref[...])
    o_ref[...] = (pw * lax.rsqrt(ms + EPS) * out_g).astype(BF16)


def _convmod(h, dw_w, vecs, pw_bf, gmat):
    b, s, ch = h.shape
    t = ROW_BLOCK
    hb = t // HALO
    n_halo = s // HALO
    return pl.pallas_call(
        _convmod_kernel,
        out_shape=jax.ShapeDtypeStruct((b, s, ch), BF16),
        grid=(b, s // t),
        in_specs=[pl.BlockSpec((None, HALO, ch), lambda bi, i: (bi, jnp.maximum(i * hb - 1, 0), 0)),
                  pl.BlockSpec((None, t, ch), lambda bi, i: (bi, i, 0)),
                  pl.BlockSpec((None, HALO, ch), lambda bi, i: (bi, jnp.minimum((i + 1) * hb, n_halo - 1), 0)),
                  pl.BlockSpec(dw_w.shape, lambda bi, i: (0, 0)),
                  pl.BlockSpec(vecs.shape, lambda bi, i: (0, 0)),
                  pl.BlockSpec(pw_bf.shape, lambda bi, i: (0, 0)),
                  pl.BlockSpec(gmat.shape, lambda bi, i: (0, 0))],
        out_specs=pl.BlockSpec((None, t, ch), lambda bi, i: (bi, i, 0)),
        scratch_shapes=[pltpu.VMEM((t + 2 * HALO, ch), F32),
                        pltpu.VMEM((t, ch), BF16)],
        compiler_params=_params("parallel", "parallel"),
        name="convmod",
    )(h, h, h, dw_w, vecs, pw_bf, gmat)


def _outproj_kernel(alpha, x_ref, a_ref, c_ref, mod_ref, wa_ref, wc_ref, ln_ref, x1_ref, u2_ref, mix_ref):
    t = x_ref.shape[0]
    half = t // 2
    for r in (0, half):
        mix_ref[r:r + half, :] = (
            jnp.dot(a_ref[r:r + half, :], wa_ref[...], preferred_element_type=F32) +
            jnp.dot(c_ref[r:r + half, :], wc_ref[...], preferred_element_type=F32))
    gate1, shift2, scale2 = mod_ref[2:3, :], mod_ref[3:4, :], mod_ref[4:5, :]
    for r in range(0, t, NORM_ROWS):
        y = alpha * x_ref[r:r + NORM_ROWS, :] + gate1 * mix_ref[r:r + NORM_ROWS, :]
        x1 = _ln_noaffine(y) * ln_ref[0:1, :] + ln_ref[1:2, :]
        x1_ref[r:r + NORM_ROWS, :] = x1
        u2_ref[r:r + NORM_ROWS, :] = (_ln_noaffine(x1) * (1.0 + scale2) + shift2).astype(BF16)


def _outproj(alpha, x, attn, hc, mod, wo_a, wo_c, ln1):
    b, s, d = x.shape
    t = ROW_BLOCK
    aw, cw = attn.shape[2], hc.shape[2]
    return pl.pallas_call(
        functools.partial(_outproj_kernel, alpha),
        out_shape=(jax.ShapeDtypeStruct((b, s, d), F32),
                   jax.ShapeDtypeStruct((b, s, d), BF16)),
        grid=(b, s // t),
        in_specs=[pl.BlockSpec((None, t, d), lambda bi, i: (bi, i, 0)),
                  pl.BlockSpec((None, t, aw), lambda bi, i: (bi, i, 0)),
                  pl.BlockSpec((None, t, cw), lambda bi, i: (bi, i, 0)),
                  pl.BlockSpec((None, 6, d), lambda bi, i: (bi, 0, 0)),
                  pl.BlockSpec((aw, d), lambda bi, i: (0, 0)),
                  pl.BlockSpec((cw, d), lambda bi, i: (0, 0)),
                  pl.BlockSpec((2, d), lambda bi, i: (0, 0))],
        out_specs=(pl.BlockSpec((None, t, d), lambda bi, i: (bi, i, 0)),
                   pl.BlockSpec((None, t, d), lambda bi, i: (bi, i, 0))),
        scratch_shapes=[pltpu.VMEM((t, d), F32)],
        compiler_params=_params("parallel", "parallel"),
        name="outproj",
    )(x, attn, hc, mod, wo_a, wo_c, ln1)


def _ffn_kernel(alpha, prev_ref, cur_ref, next_ref, x1_ref, mod_ref, wup_ref, dw_ref, wdn_ref, ln_ref,
                o_ref, ext_ref, hv_ref, hg_ref, acc_ref):
    i = pl.program_id(1)
    last = pl.num_programs(1) - 1
    t = cur_ref.shape[0]
    d_ff = wdn_ref.shape[0]
    ext_ref[0:HALO, :] = jnp.where(i > 0, prev_ref[...], jnp.zeros_like(prev_ref[...]))
    ext_ref[HALO:HALO + t, :] = cur_ref[...]
    ext_ref[HALO + t:, :] = jnp.where(i < last, next_ref[...], jnp.zeros_like(next_ref[...]))
    ext = ext_ref[...]

    piece = t // FFN_PIECES

    def conv3(h_ref, slot, col, r0):
        w = dw_ref[:, col:col + FF_CHUNK]
        lo = HALO + r0
        return (w[0:1] * h_ref[slot, lo - 1:lo - 1 + piece, :] + w[1:2] * h_ref[slot, lo:lo + piece, :] +
                w[2:3] * h_ref[slot, lo + 1:lo + 1 + piece, :] + w[3:4])

    def up_val(c, slot):
        hv_ref[slot] = jnp.dot(ext, wup_ref[:, c:c + FF_CHUNK], preferred_element_type=F32)

    def up_gate(c, slot):
        hg_ref[slot] = jnp.dot(ext, wup_ref[:, d_ff + c:d_ff + c + FF_CHUNK], preferred_element_type=F32)

    issue_before = {0: up_val, FFN_PIECES // 2: up_gate}
    up_val(0, 0)
    up_gate(0, 0)
    acc_ref[...] = jnp.zeros(acc_ref.shape, F32)
    for n, c in enumerate(range(0, d_ff, FF_CHUNK)):
        slot = n % 2
        for pi in range(FFN_PIECES):
            if c + FF_CHUNK < d_ff and pi in issue_before:
                issue_before[pi](c + FF_CHUNK, 1 - slot)
            r0 = pi * piece
            val = conv3(hv_ref, slot, c, r0)
            gt = conv3(hg_ref, slot, d_ff + c, r0)
            act = 0.5 * gt * (1.0 + lax.erf(gt * (2.0 ** -0.5))) * val
            acc_ref[r0:r0 + piece, :] += jnp.dot(act.astype(BF16), wdn_ref[c:c + FF_CHUNK, :],
                                                 preferred_element_type=F32)

    gate2 = mod_ref[5:6, :]
    for r in range(0, t, NORM_ROWS):
        y = alpha * x1_ref[r:r + NORM_ROWS, :] + gate2 * acc_ref[r:r + NORM_ROWS, :]
        o_ref[r:r + NORM_ROWS, :] = _ln_noaffine(y) * ln_ref[0:1, :] + ln_ref[1:2, :]


def _ffn(alpha, u2, x1, mod, wup_bf, dw4, wdn_bf, ln2):
    b, s, d = x1.shape
    t = ROW_BLOCK
    hb = t // HALO
    n_halo = s // HALO
    resident = dict(pipeline_mode=pl.Buffered(1))
    return pl.pallas_call(
        functools.partial(_ffn_kernel, alpha),
        out_shape=jax.ShapeDtypeStruct((b, s, d), F32),
        grid=(b, s // t),
        in_specs=[pl.BlockSpec((None, HALO, d), lambda bi, i: (bi, jnp.maximum(i * hb - 1, 0), 0)),
                  pl.BlockSpec((None, t, d), lambda bi, i: (bi, i, 0)),
                  pl.BlockSpec((None, HALO, d), lambda bi, i: (bi, jnp.minimum((i + 1) * hb, n_halo - 1), 0)),
                  pl.BlockSpec((None, t, d), lambda bi, i: (bi, i, 0)),
                  pl.BlockSpec((None, 6, d), lambda bi, i: (bi, 0, 0)),
                  pl.BlockSpec(wup_bf.shape, lambda bi, i: (0, 0), **resident),
                  pl.BlockSpec(dw4.shape, lambda bi, i: (0, 0)),
                  pl.BlockSpec(wdn_bf.shape, lambda bi, i: (0, 0), **resident),
                  pl.BlockSpec((2, d), lambda bi, i: (0, 0))],
        out_specs=pl.BlockSpec((None, t, d), lambda bi, i: (bi, i, 0)),
        scratch_shapes=[pltpu.VMEM((t + 2 * HALO, d), BF16),
                        pltpu.VMEM((2, t + 2 * HALO, FF_CHUNK), F32),
                        pltpu.VMEM((2, t + 2 * HALO, FF_CHUNK), F32),
                        pltpu.VMEM((t, d), F32)],
        compiler_params=_params("parallel", "parallel"),
        name="ffn",
    )(u2, u2, u2, x1, mod, wup_bf, dw4, wdn_bf, ln2)


def _rope_table_t(seq_len):
    pos = jnp.arange(seq_len, dtype=jnp.int32)
    rows = (pos // GRID_W).astype(F32)
    cols = (pos % GRID_W).astype(F32)
    inv_freq = ROPE_THETA ** (-jnp.arange(ROPE_FREQS, dtype=F32) / ROPE_FREQS)
    ang_r = inv_freq[:, None] * rows[None, :]
    ang_c = inv_freq[:, None] * cols[None, :]
    return jnp.concatenate([jnp.cos(ang_r), jnp.sin(ang_r), jnp.cos(ang_c), jnp.sin(ang_c)], axis=0)


def kernel(x, c, w_ada, b_ada, w_in, q_norm_g, k_norm_g, conv_dw_w, conv_dw_b, conv_ln_g, conv_ln_b,
           w_conv_pw2, attn_out_g, conv_out_g, w_o, ln1_g, ln1_b, w_up, ffn_dw_w, ffn_dw_b, w_down,
           ln2_g, ln2_b):
    b, s, d = x.shape
    depth = w_ada.shape[0]
    alpha = (2.0 * depth) ** 0.25
    conv_ch = w_conv_pw2.shape[1]
    assert s % ROW_BLOCK == 0 and s % Q_BLOCK == 0 and s % GRID_W == 0
    assert w_down.shape[1] % FF_CHUNK == 0 and b <= 8

    tab = _rope_table_t(s)
    c_pad = jnp.zeros((8, d), F32).at[:b].set(c)
    gidx = jnp.arange(conv_ch) // CONV_GROUP_DIM
    gmat = jnp.where(gidx[:, None] == gidx[None, :], 1.0 / CONV_GROUP_DIM, 0.0).astype(BF16)

    for l in range(depth):
        mod = _adaln(c_pad, w_ada[l], b_ada[l][None, :])[:b].reshape(b, 6, d)

        gains = jnp.stack([jnp.broadcast_to(q_norm_g[l][:, None], (HEAD_DIM, ROW_BLOCK)),
                           jnp.broadcast_to(k_norm_g[l][:, None], (HEAD_DIM, ROW_BLOCK))])
        qt, k, vt4, h, kn = _inproj(x, mod, w_in[l].astype(BF16), gains, tab)

        k4 = k.reshape(b, s // KV_CHUNK, KV_CHUNK, KV_WIDTH)
        gain_t = jnp.broadcast_to(attn_out_g[l].reshape(N_KV_HEADS, GROUPS * HEAD_DIM, 1),
                                  (N_KV_HEADS, GROUPS * HEAD_DIM, Q_BLOCK))
        attn = _attention(qt, k4, vt4, kn.reshape(b, N_KV_HEADS, 1, s), gain_t)

        dw_w = jnp.zeros((CONV_WIDTH + 1, conv_ch), F32).at[:CONV_WIDTH].set(conv_dw_w[l])
        vecs = jnp.stack([conv_dw_b[l], conv_ln_g[l], conv_ln_b[l], conv_out_g[l].reshape(-1)])
        hc = _convmod(h, dw_w, vecs, w_conv_pw2[l].astype(BF16), gmat)

        wo_bf = w_o[l].astype(BF16)
        x1, u2 = _outproj(alpha, x, attn, hc, mod, wo_bf[:ATTN_WIDTH], wo_bf[ATTN_WIDTH:],
                          jnp.stack([ln1_g[l], ln1_b[l]]))

        dw4 = jnp.concatenate([ffn_dw_w[l], ffn_dw_b[l][None, :]], axis=0)
        x = _ffn(alpha, u2, x1, mod, w_up[l].astype(BF16), dw4, w_down[l].astype(BF16),
                 jnp.stack([ln2_g[l], ln2_b[l]]))
    return x
```

```python
import functools

import jax
import jax.numpy as jnp
from jax import lax
from jax.experimental import pallas as pl
from jax.experimental.pallas import tpu as pltpu

GRID_W = 64
HEAD_DIM = 64
N_HEADS = 8
N_KV_HEADS = 2
GROUPS = N_HEADS // N_KV_HEADS
ATTN_WIDTH = N_HEADS * HEAD_DIM
KV_WIDTH = N_KV_HEADS * HEAD_DIM
CONV_GROUP_DIM = 64
CONV_WIDTH = 31
CONV_PAD = (CONV_WIDTH - 1) // 2
FFN_CONV_WIDTH = 3
ROPE_THETA = 10000.0
ROPE_FREQS = HEAD_DIM // 4
EPS = 1e-6
Q_SCALE = HEAD_DIM ** -0.5 * 1.4426950408889634
SUM_ROWS = 16
V_ROWS = HEAD_DIM + SUM_ROWS

F32 = jnp.float32
BF16 = jnp.bfloat16

V7X_VMEM_LIMIT_BYTES = 56 * 1024 * 1024
ROW_BLOCK = 512
Q_BLOCK = 512
KV_CHUNK = ROW_BLOCK
LANES = 128
HALO = 16
CONV_ROWS = 64
FF_CHUNK = 256
FFN_PIECES = 4
NORM_ROWS = 32
MAX_SLAB = 64
SCORES_AHEAD = 2
CHUNK_UNROLL = 4
MAX_SAFE_BOUND = 60.0
BOUND_SLACK = 1.0 + 2.0 ** -6


def _params(*sem):
    return pltpu.CompilerParams(dimension_semantics=sem, vmem_limit_bytes=V7X_VMEM_LIMIT_BYTES)


def _ln_noaffine(x):
    mu = jnp.mean(x, axis=-1, keepdims=True)
    xc = x - mu
    var = jnp.mean(xc * xc, axis=-1, keepdims=True)
    return xc * lax.rsqrt(var + EPS)


def _adaln_kernel(c_ref, w_ref, b_ref, o_ref):
    c = c_ref[...]
    c_act = c * jax.nn.sigmoid(c)
    w = w_ref[...]
    c_hi, w_hi = c_act.astype(BF16), w.astype(BF16)
    c_lo = (c_act - c_hi.astype(F32)).astype(BF16)
    w_lo = (w - w_hi.astype(F32)).astype(BF16)
    o_ref[...] = (jnp.dot(c_hi, w_hi, preferred_element_type=F32) +
                  jnp.dot(c_hi, w_lo, preferred_element_type=F32) +
                  jnp.dot(c_lo, w_hi, preferred_element_type=F32)) + b_ref[...]


def _adaln(c_pad, w, b):
    rows, d = c_pad.shape
    n = w.shape[1]
    return pl.pallas_call(
        _adaln_kernel,
        out_shape=jax.ShapeDtypeStruct((rows, n), F32),
        grid=(n // d,),
        in_specs=[pl.BlockSpec((rows, d), lambda j: (0, 0)),
                  pl.BlockSpec((d, d), lambda j: (0, j)),
                  pl.BlockSpec((1, d), lambda j: (0, j))],
        out_specs=pl.BlockSpec((rows, d), lambda j: (0, j)),
        compiler_params=_params("arbitrary"),
        name="adaln",
    )(c_pad, w, b)


def _rope_t(x, tab):
    f = ROPE_FREQS
    cr, sr, cc, sc = tab[0:f], tab[f:2 * f], tab[2 * f:3 * f], tab[3 * f:4 * f]
    x1r, x2r, x1c, x2c = x[0:f], x[f:2 * f], x[2 * f:3 * f], x[3 * f:4 * f]
    return jnp.concatenate([x1r * cr - x2r * sr, x2r * cr + x1r * sr,
                            x1c * cc - x2c * sc, x2c * cc + x1c * sc], axis=0)


def _norm_rope_t(xt, gain, tab):
    ms = jnp.mean(xt * xt, axis=0, keepdims=True)
    return _rope_t(xt * lax.rsqrt(ms + EPS) * gain, tab)


def _inproj_kernel(x_ref, mod_ref, w_ref, gains_ref, tab_ref, qt_ref, k_ref, vt_ref, h_ref, kn_ref,
                   u_ref, proj_ref):
    shift, scale = mod_ref[0:1, :], mod_ref[1:2, :]
    t = x_ref.shape[0]
    half = t // 2
    qkv_w = ATTN_WIDTH + 2 * KV_WIDTH
    conv_ch = (w_ref.shape[1] - qkv_w) // 2

    for r0 in (0, half):
        for r in range(r0, r0 + half, NORM_ROWS):
            u = _ln_noaffine(x_ref[r:r + NORM_ROWS, :]) * (1.0 + scale) + shift
            u_ref[r:r + NORM_ROWS, :] = u.astype(BF16)
        proj_ref[r0:r0 + half, :] = jnp.dot(u_ref[r0:r0 + half, :], w_ref[...],
                                            preferred_element_type=F32)

    for r0 in (0, half):
        cols = slice(r0, r0 + half)
        qkv_t = proj_ref[r0:r0 + half, 0:qkv_w].T
        tab = tab_ref[:, cols]
        gq, gk = gains_ref[0, :, 0:half], gains_ref[1, :, 0:half]
        for h in range(N_HEADS):
            r = h * HEAD_DIM
            qh = _norm_rope_t(qkv_t[r:r + HEAD_DIM], gq, tab) * Q_SCALE
            qt_ref[r:r + HEAD_DIM, cols] = qh.astype(BF16)
        k_rot = []
        for h in range(N_KV_HEADS):
            r = ATTN_WIDTH + h * HEAD_DIM
            kh = _norm_rope_t(qkv_t[r:r + HEAD_DIM], gk, tab).astype(BF16)
            k_rot.append(kh)
            khf = kh.astype(F32)
            kn_ref[h:h + 1, cols] = jnp.sum(khf * khf, axis=0, keepdims=True)
        k_ref[cols, :] = jnp.concatenate(k_rot, axis=0).astype(F32).T.astype(BF16)
        for h in range(N_KV_HEADS):
            r = ATTN_WIDTH + KV_WIDTH + h * HEAD_DIM
            vt_ref[h, 0:HEAD_DIM, cols] = qkv_t[r:r + HEAD_DIM].astype(BF16)
            vt_ref[h, HEAD_DIM:, cols] = jnp.ones((SUM_ROWS, half), BF16)

        a = proj_ref[r0:r0 + half, qkv_w:qkv_w + conv_ch]
        g = proj_ref[r0:r0 + half, qkv_w + conv_ch:]
        h_ref[r0:r0 + half, :] = a * jax.nn.sigmoid(g)


def _inproj(x, mod, w_bf, gains, tab):
    b, s, d = x.shape
    t = ROW_BLOCK
    n_cols = w_bf.shape[1]
    conv_ch = (n_cols - ATTN_WIDTH - 2 * KV_WIDTH) // 2
    return pl.pallas_call(
        _inproj_kernel,
        out_shape=(jax.ShapeDtypeStruct((b, ATTN_WIDTH, s), BF16),
                   jax.ShapeDtypeStruct((b, s, KV_WIDTH), BF16),
                   jax.ShapeDtypeStruct((b, s // t, N_KV_HEADS, V_ROWS, t), BF16),
                   jax.ShapeDtypeStruct((b, s, conv_ch), F32),
                   jax.ShapeDtypeStruct((b, N_KV_HEADS, s), F32)),
        grid=(b, s // t),
        in_specs=[pl.BlockSpec((None, t, d), lambda bi, i: (bi, i, 0)),
                  pl.BlockSpec((None, 6, d), lambda bi, i: (bi, 0, 0)),
                  pl.BlockSpec((d, n_cols), lambda bi, i: (0, 0)),
                  pl.BlockSpec((2, HEAD_DIM, t), lambda bi, i: (0, 0, 0)),
                  pl.BlockSpec((HEAD_DIM, t), lambda bi, i: (0, i))],
        out_specs=(pl.BlockSpec((None, ATTN_WIDTH, t), lambda bi, i: (bi, 0, i)),
                   pl.BlockSpec((None, t, KV_WIDTH), lambda bi, i: (bi, i, 0)),
                   pl.BlockSpec((None, None, N_KV_HEADS, V_ROWS, t), lambda bi, i: (bi, i, 0, 0, 0)),
                   pl.BlockSpec((None, t, conv_ch), lambda bi, i: (bi, i, 0)),
                   pl.BlockSpec((None, N_KV_HEADS, t), lambda bi, i: (bi, 0, i))),
        scratch_shapes=[pltpu.VMEM((t, d), BF16),
                        pltpu.VMEM((t, n_cols), F32)],
        compiler_params=_params("parallel", "parallel"),
        name="inproj",
    )(x, mod, w_bf, gains, tab)


def _attn_kernel(qt_ref, k_ref, vt_ref, kn_ref, g_ref, o_ref, qp_ref, s_ref, p_ref, m_ref, acc_ref):
    grp = pl.program_id(1)
    n_chunks, tk = k_ref.shape[0], k_ref.shape[1]
    tq = qt_ref.shape[1]

    first = grp == 0
    k_norm2_max = jnp.max(kn_ref[...], axis=1, keepdims=True)
    bounds = []
    for h in range(GROUPS):
        qh = qt_ref[h * HEAD_DIM:(h + 1) * HEAD_DIM, :]
        z = jnp.zeros_like(qh)
        qp_ref[h, 0:HEAD_DIM, :] = jnp.where(first, qh, z)
        qp_ref[h, HEAD_DIM:2 * HEAD_DIM, :] = jnp.where(first, z, qh)
        qf = qh.astype(F32)
        q_norm2 = jnp.sum(qf * qf, axis=0, keepdims=True)
        bounds.append(jnp.sqrt(q_norm2 * k_norm2_max) * BOUND_SLACK)
    acc_ref[...] = jnp.zeros(acc_ref.shape, F32)
    bound_max = jnp.max(jnp.concatenate(bounds, axis=0))

    def scores(c, h):
        return jnp.dot(k_ref[c], qp_ref[h], preferred_element_type=F32)

    @pl.when(bound_max <= MAX_SAFE_BOUND)
    def _():
        for h in range(GROUPS):
            m_ref[h] = bounds[h]

        def probs(c, h, slot):
            p_ref[slot] = jnp.exp2(scores(c, h) - m_ref[h]).astype(BF16)

        probs(0, 0, 0)

        def chunk(c, carry):
            vc = vt_ref[c]
            c_next = jnp.minimum(c + 1, n_chunks - 1)
            for h in range(GROUPS):
                if h + 1 < GROUPS:
                    probs(c, h + 1, (h + 1) % 2)
                else:
                    probs(c_next, 0, 0)
                acc_ref[h] += jnp.dot(vc, p_ref[h % 2], preferred_element_type=F32)
            return carry

        lax.fori_loop(0, n_chunks, chunk, 0, unroll=CHUNK_UNROLL)

    @pl.when(jnp.logical_not(bound_max <= MAX_SAFE_BOUND))
    def _():
        m_ref[...] = jnp.full(m_ref.shape, -jnp.inf, F32)
        for h in range(SCORES_AHEAD):
            s_ref[h] = scores(0, h)

        def chunk(c, carry):
            vc = vt_ref[c]
            c_next = jnp.minimum(c + 1, n_chunks - 1)
            for h in range(GROUPS):
                ahead = h + SCORES_AHEAD
                if ahead < GROUPS:
                    s_ref[ahead] = scores(c, ahead)
                else:
                    s_ref[ahead - GROUPS] = scores(c_next, ahead - GROUPS)
                s = s_ref[h]
                m_old = m_ref[h]
                slab_max = jnp.max(s.reshape(tk // MAX_SLAB, MAX_SLAB, tq), axis=0)
                m_new = jnp.maximum(m_old, jnp.max(slab_max, axis=0, keepdims=True))
                p = jnp.exp2(s - m_new)
                alpha = jnp.exp2(m_old - m_new)
                pv = jnp.dot(vc, p.astype(BF16), preferred_element_type=F32)
                acc_ref[h] = alpha * acc_ref[h] + pv
                m_ref[h] = m_new
            return carry

        lax.fori_loop(0, n_chunks, chunk, 0, unroll=2)

    outs = []
    for h in range(GROUPS):
        o = acc_ref[h, 0:HEAD_DIM, :] / acc_ref[h, HEAD_DIM:HEAD_DIM + 1, :]
        ms = jnp.mean(o * o, axis=0, keepdims=True)
        outs.append(o * lax.rsqrt(ms + EPS) * g_ref[h * HEAD_DIM:(h + 1) * HEAD_DIM, :])
    o_ref[...] = jnp.concatenate(outs, axis=0).T.astype(BF16)


def _attention(qt, k4, vt4, kn4, gain_t):
    b, _, s = qt.shape
    n_chunks, tk = k4.shape[1], k4.shape[2]
    tq = Q_BLOCK
    gw = GROUPS * HEAD_DIM
    return pl.pallas_call(
        _attn_kernel,
        out_shape=jax.ShapeDtypeStruct((b, s, ATTN_WIDTH), BF16),
        grid=(b, N_KV_HEADS, s // tq),
        in_specs=[pl.BlockSpec((None, gw, tq), lambda bi, g, qi: (bi, g, qi)),
                  pl.BlockSpec((None, n_chunks, tk, KV_WIDTH), lambda bi, g, qi: (bi, 0, 0, 0)),
                  pl.BlockSpec((None, n_chunks, None, V_ROWS, tk), lambda bi, g, qi: (bi, 0, g, 0, 0)),
                  pl.BlockSpec((None, None, 1, s), lambda bi, g, qi: (bi, g, 0, 0)),
                  pl.BlockSpec((None, gw, tq), lambda bi, g, qi: (g, 0, 0))],
        out_specs=pl.BlockSpec((None, tq, gw), lambda bi, g, qi: (bi, qi, g)),
        scratch_shapes=[pltpu.VMEM((GROUPS, 2 * HEAD_DIM, tq), BF16),
                        pltpu.VMEM((GROUPS, tk, tq), F32),
                        pltpu.VMEM((2, tk, tq), BF16),
                        pltpu.VMEM((GROUPS, 1, tq), F32),
                        pltpu.VMEM((GROUPS, V_ROWS, tq), F32)],
        compiler_params=_params("parallel", "parallel", "parallel"),
        name="attn",
    )(qt, k4, vt4, kn4, gain_t)


def _group_mean(sq, gmat):
    hi = sq.astype(BF16)
    lo = (sq - hi.astype(F32)).astype(BF16)
    return (jnp.dot(hi, gmat, preferred_element_type=F32) +
            jnp.dot(lo, gmat, preferred_element_type=F32))


def _convmod_kernel(prev_ref, cur_ref, next_ref, dw_ref, vec_ref, pw_ref, gmat_ref, o_ref, ext_ref, z_ref):
    i = pl.program_id(1)
    last = pl.num_programs(1) - 1
    t = cur_ref.shape[0]
    n_slab = cur_ref.shape[1] // LANES
    prev = jnp.where(i > 0, prev_ref[...], 0.0)
    nxt = jnp.where(i < last, next_ref[...], 0.0)
    for sl in range(n_slab):
        lanes = slice(sl * LANES, (sl + 1) * LANES)
        ext_ref[sl, 0:HALO, :] = prev[:, lanes]
        ext_ref[sl, HALO:HALO + t, :] = cur_ref[:, lanes]
        ext_ref[sl, HALO + t:, :] = nxt[:, lanes]

    dw_b, ln_g, ln_b, out_g = vec_ref[0:1, :], vec_ref[1:2, :], vec_ref[2:3, :], vec_ref[3:4, :]
    base = HALO - CONV_PAD
    rows_par = CONV_ROWS // 2
    for r in range(0, t, CONV_ROWS):
        for parity in range(2):
            slabs = []
            for sl in range(n_slab):
                lanes = slice(sl * LANES, (sl + 1) * LANES)
                acc = jnp.broadcast_to(dw_b[:, lanes], (rows_par, LANES))
                for j in range(CONV_WIDTH):
                    window = ext_ref[sl, pl.ds(r + parity + j + base, rows_par, stride=2), :]
                    acc = acc + dw_ref[j:j + 1, lanes] * window
                slabs.append(acc)
            y = _ln_noaffine(jnp.concatenate(slabs, axis=1)) * ln_g + ln_b
            z = y * jax.nn.sigmoid(y)
            for sl in range(n_slab):
                z_ref[sl, pl.ds(r + parity, rows_par, stride=2), :] = z[:, sl * LANES:(sl + 1) * LANES]

    z_rows = jnp.concatenate([z_ref[sl] for sl in range(n_slab)], axis=1).astype(BF16)
    pw = jnp.dot(z_rows, pw_ref[...], preferred_element_type=F32)
    ms = _group_mean(pw * pw, gmat_ref[...])
    o_ref[...] = (pw * lax.rsqrt(ms + EPS) * out_g).astype(BF16)


def _convmod(h, dw_w, vecs, pw_bf, gmat):
    b, s, ch = h.shape
    t = ROW_BLOCK
    hb = t // HALO
    n_halo = s // HALO
    return pl.pallas_call(
        _convmod_kernel,
        out_shape=jax.ShapeDtypeStruct((b, s, ch), BF16),
        grid=(b, s // t),
        in_specs=[pl.BlockSpec((None, HALO, ch), lambda bi, i: (bi, jnp.maximum(i * hb - 1, 0), 0)),
                  pl.BlockSpec((None, t, ch), lambda bi, i: (bi, i, 0)),
                  pl.BlockSpec((None, HALO, ch), lambda bi, i: (bi, jnp.minimum((i + 1) * hb, n_halo - 1), 0)),
                  pl.BlockSpec(dw_w.shape, lambda bi, i: (0, 0)),
                  pl.BlockSpec(vecs.shape, lambda bi, i: (0, 0)),
                  pl.BlockSpec(pw_bf.shape, lambda bi, i: (0, 0)),
                  pl.BlockSpec(gmat.shape, lambda bi, i: (0, 0))],
        out_specs=pl.BlockSpec((None, t, ch), lambda bi, i: (bi, i, 0)),
        scratch_shapes=[pltpu.VMEM((ch // LANES, t + 2 * HALO, LANES), F32),
                        pltpu.VMEM((ch // LANES, t, LANES), F32)],
        compiler_params=_params("parallel", "parallel"),
        name="convmod",
    )(h, h, h, dw_w, vecs, pw_bf, gmat)


def _outproj_kernel(alpha, x_ref, a_ref, c_ref, mod_ref, wa_ref, wc_ref, ln_ref, x1_ref, u2_ref, mix_ref):
    t = x_ref.shape[0]
    half = t // 2
    for r in (0, half):
        mix_ref[r:r + half, :] = (
            jnp.dot(a_ref[r:r + half, :], wa_ref[...], preferred_element_type=F32) +
            jnp.dot(c_ref[r:r + half, :], wc_ref[...], preferred_element_type=F32))
    gate1, shift2, scale2 = mod_ref[2:3, :], mod_ref[3:4, :], mod_ref[4:5, :]
    for r in range(0, t, NORM_ROWS):
        y = alpha * x_ref[r:r + NORM_ROWS, :] + gate1 * mix_ref[r:r + NORM_ROWS, :]
        x1 = _ln_noaffine(y) * ln_ref[0:1, :] + ln_ref[1:2, :]
        x1_ref[r:r + NORM_ROWS, :] = x1
        u2_ref[r:r + NORM_ROWS, :] = (_ln_noaffine(x1) * (1.0 + scale2) + shift2).astype(BF16)


def _outproj(alpha, x, attn, hc, mod, wo_a, wo_c, ln1):
    b, s, d = x.shape
    t = ROW_BLOCK
    aw, cw = attn.shape[2], hc.shape[2]
    return pl.pallas_call(
        functools.partial(_outproj_kernel, alpha),
        out_shape=(jax.ShapeDtypeStruct((b, s, d), F32),
                   jax.ShapeDtypeStruct((b, s, d), BF16)),
        grid=(b, s // t),
        in_specs=[pl.BlockSpec((None, t, d), lambda bi, i: (bi, i, 0)),
                  pl.BlockSpec((None, t, aw), lambda bi, i: (bi, i, 0)),
                  pl.BlockSpec((None, t, cw), lambda bi, i: (bi, i, 0)),
                  pl.BlockSpec((None, 6, d), lambda bi, i: (bi, 0, 0)),
                  pl.BlockSpec((aw, d), lambda bi, i: (0, 0)),
                  pl.BlockSpec((cw, d), lambda bi, i: (0, 0)),
                  pl.BlockSpec((2, d), lambda bi, i: (0, 0))],
        out_specs=(pl.BlockSpec((None, t, d), lambda bi, i: (bi, i, 0)),
                   pl.BlockSpec((None, t, d), lambda bi, i: (bi, i, 0))),
        scratch_shapes=[pltpu.VMEM((t, d), F32)],
        compiler_params=_params("parallel", "parallel"),
        name="outproj",
    )(x, attn, hc, mod, wo_a, wo_c, ln1)


def _ffn_kernel(alpha, prev_ref, cur_ref, next_ref, x1_ref, mod_ref, wup_ref, dw_ref, wdn_ref, ln_ref,
                o_ref, ext_ref, hv_ref, hg_ref, act_ref, acc_ref):
    i = pl.program_id(1)
    last = pl.num_programs(1) - 1
    t = cur_ref.shape[0]
    d_ff = wdn_ref.shape[0]
    ext_ref[0:HALO, :] = jnp.where(i > 0, prev_ref[...], jnp.zeros_like(prev_ref[...]))
    ext_ref[HALO:HALO + t, :] = cur_ref[...]
    ext_ref[HALO + t:, :] = jnp.where(i < last, next_ref[...], jnp.zeros_like(next_ref[...]))
    ext = ext_ref[...]

    piece = t // FFN_PIECES
    rows_par = piece // 2
    n_slab = FF_CHUNK // LANES

    def conv3(h_ref, slot, sl, col, first):
        w = dw_ref[:, col + sl * LANES:col + (sl + 1) * LANES]
        taps = [h_ref[slot, sl, pl.ds(first + k, rows_par, stride=2), :] for k in range(FFN_CONV_WIDTH)]
        return w[0:1] * taps[0] + w[1:2] * taps[1] + w[2:3] * taps[2] + w[3:4]

    def up_into(h_ref, c, slot):
        r = jnp.dot(ext, wup_ref[:, c:c + FF_CHUNK], preferred_element_type=F32)
        for sl in range(n_slab):
            h_ref[slot, sl] = r[:, sl * LANES:(sl + 1) * LANES]

    def up_val(c, slot):
        up_into(hv_ref, c, slot)

    def up_gate(c, slot):
        up_into(hg_ref, d_ff + c, slot)

    issue_before = {0: up_val, FFN_PIECES // 2: up_gate}
    up_val(0, 0)
    up_gate(0, 0)
    acc_ref[...] = jnp.zeros(acc_ref.shape, F32)
    for n, c in enumerate(range(0, d_ff, FF_CHUNK)):
        slot = n % 2
        for pi in range(FFN_PIECES):
            if c + FF_CHUNK < d_ff and pi in issue_before:
                issue_before[pi](c + FF_CHUNK, 1 - slot)
            r0 = pi * piece
            for sl in range(n_slab):
                for parity in range(2):
                    first = HALO + r0 + parity - 1
                    val = conv3(hv_ref, slot, sl, c, first)
                    gt = conv3(hg_ref, slot, sl, d_ff + c, first)
                    act = 0.5 * gt * (1.0 + lax.erf(gt * (2.0 ** -0.5))) * val
                    act_ref[sl, pl.ds(r0 + parity, rows_par, stride=2), :] = act
            act_rows = jnp.concatenate([act_ref[sl, r0:r0 + piece, :] for sl in range(n_slab)], axis=1)
            acc_ref[r0:r0 + piece, :] += jnp.dot(act_rows.astype(BF16), wdn_ref[c:c + FF_CHUNK, :],
                                                 preferred_element_type=F32)

    gate2 = mod_ref[5:6, :]
    for r in range(0, t, NORM_ROWS):
        y = alpha * x1_ref[r:r + NORM_ROWS, :] + gate2 * acc_ref[r:r + NORM_ROWS, :]
        o_ref[r:r + NORM_ROWS, :] = _ln_noaffine(y) * ln_ref[0:1, :] + ln_ref[1:2, :]


def _ffn(alpha, u2, x1, mod, wup_bf, dw4, wdn_bf, ln2):
    b, s, d = x1.shape
    t = ROW_BLOCK
    hb = t // HALO
    n_halo = s // HALO
    resident = dict(pipeline_mode=pl.Buffered(1))
    return pl.pallas_call(
        functools.partial(_ffn_kernel, alpha),
        out_shape=jax.ShapeDtypeStruct((b, s, d), F32),
        grid=(b, s // t),
        in_specs=[pl.BlockSpec((None, HALO, d), lambda bi, i: (bi, jnp.maximum(i * hb - 1, 0), 0)),
                  pl.BlockSpec((None, t, d), lambda bi, i: (bi, i, 0)),
                  pl.BlockSpec((None, HALO, d), lambda bi, i: (bi, jnp.minimum((i + 1) * hb, n_halo - 1), 0)),
                  pl.BlockSpec((None, t, d), lambda bi, i: (bi, i, 0)),
                  pl.BlockSpec((None, 6, d), lambda bi, i: (bi, 0, 0)),
                  pl.BlockSpec(wup_bf.shape, lambda bi, i: (0, 0), **resident),
                  pl.BlockSpec(dw4.shape, lambda bi, i: (0, 0)),
                  pl.BlockSpec(wdn_bf.shape, lambda bi, i: (0, 0), **resident),
                  pl.BlockSpec((2, d), lambda bi, i: (0, 0))],
        out_specs=pl.BlockSpec((None, t, d), lambda bi, i: (bi, i, 0)),
        scratch_shapes=[pltpu.VMEM((t + 2 * HALO, d), BF16),
                        pltpu.VMEM((2, FF_CHUNK // LANES, t + 2 * HALO, LANES), F32),
                        pltpu.VMEM((2, FF_CHUNK // LANES, t + 2 * HALO, LANES), F32),
                        pltpu.VMEM((FF_CHUNK // LANES, t, LANES), F32),
                        pltpu.VMEM((t, d), F32)],
        compiler_params=_params("parallel", "parallel"),
        name="ffn",
    )(u2, u2, u2, x1, mod, wup_bf, dw4, wdn_bf, ln2)


def _rope_table_t(seq_len):
    pos = jnp.arange(seq_len, dtype=jnp.int32)
    rows = (pos // GRID_W).astype(F32)
    cols = (pos % GRID_W).astype(F32)
    inv_freq = ROPE_THETA ** (-jnp.arange(ROPE_FREQS, dtype=F32) / ROPE_FREQS)
    ang_r = inv_freq[:, None] * rows[None, :]
    ang_c = inv_freq[:, None] * cols[None, :]
    return jnp.concatenate([jnp.cos(ang_r), jnp.sin(ang_r), jnp.cos(ang_c), jnp.sin(ang_c)], axis=0)


def kernel(x, c, w_ada, b_ada, w_in, q_norm_g, k_norm_g, conv_dw_w, conv_dw_b, conv_ln_g, conv_ln_b,
           w_conv_pw2, attn_out_g, conv_out_g, w_o, ln1_g, ln1_b, w_up, ffn_dw_w, ffn_dw_b, w_down,
           ln2_g, ln2_b):
    b, s, d = x.shape
    depth = w_ada.shape[0]
    alpha = (2.0 * depth) ** 0.25
    conv_ch = w_conv_pw2.shape[1]
    assert s % ROW_BLOCK == 0 and s % Q_BLOCK == 0 and s % GRID_W == 0
    assert w_down.shape[1] % FF_CHUNK == 0 and b <= 8

    tab = _rope_table_t(s)
    c_pad = jnp.zeros((8, d), F32).at[:b].set(c)
    gidx = jnp.arange(conv_ch) // CONV_GROUP_DIM
    gmat = jnp.where(gidx[:, None] == gidx[None, :], 1.0 / CONV_GROUP_DIM, 0.0).astype(BF16)

    for l in range(depth):
        mod = _adaln(c_pad, w_ada[l], b_ada[l][None, :])[:b].reshape(b, 6, d)

        gains = jnp.stack([jnp.broadcast_to(q_norm_g[l][:, None], (HEAD_DIM, ROW_BLOCK)),
                           jnp.broadcast_to(k_norm_g[l][:, None], (HEAD_DIM, ROW_BLOCK))])
        qt, k, vt4, h, kn = _inproj(x, mod, w_in[l].astype(BF16), gains, tab)

        k4 = k.reshape(b, s // KV_CHUNK, KV_CHUNK, KV_WIDTH)
        gain_t = jnp.broadcast_to(attn_out_g[l].reshape(N_KV_HEADS, GROUPS * HEAD_DIM, 1),
                                  (N_KV_HEADS, GROUPS * HEAD_DIM, Q_BLOCK))
        attn = _attention(qt, k4, vt4, kn.reshape(b, N_KV_HEADS, 1, s), gain_t)

        dw_w = jnp.zeros((CONV_WIDTH + 1, conv_ch), F32).at[:CONV_WIDTH].set(conv_dw_w[l])
        vecs = jnp.stack([conv_dw_b[l], conv_ln_g[l], conv_ln_b[l], conv_out_g[l].reshape(-1)])
        hc = _convmod(h, dw_w, vecs, w_conv_pw2[l].astype(BF16), gmat)

        wo_bf = w_o[l].astype(BF16)
        x1, u2 = _outproj(alpha, x, attn, hc, mod, wo_bf[:ATTN_WIDTH], wo_bf[ATTN_WIDTH:],
                          jnp.stack([ln1_g[l], ln1_b[l]]))

        dw4 = jnp.concatenate([ffn_dw_w[l], ffn_dw_b[l][None, :]], axis=0)
        x = _ffn(alpha, u2, x1, mod, w_up[l].astype(BF16), dw4, w_down[l].astype(BF16),
                 jnp.stack([ln2_g[l], ln2_b[l]]))
    return x
```

```python
import functools

import jax
import jax.numpy as jnp
from jax import lax
from jax.experimental import pallas as pl
from jax.experimental.pallas import tpu as pltpu

GRID_W = 64
HEAD_DIM = 64
N_HEADS = 8
N_KV_HEADS = 2
GROUPS = N_HEADS // N_KV_HEADS
ATTN_WIDTH = N_HEADS * HEAD_DIM
KV_WIDTH = N_KV_HEADS * HEAD_DIM
CONV_GROUP_DIM = 64
CONV_WIDTH = 31
CONV_PAD = (CONV_WIDTH - 1) // 2
FFN_CONV_WIDTH = 3
ROPE_THETA = 10000.0
ROPE_FREQS = HEAD_DIM // 4
EPS = 1e-6
Q_SCALE = HEAD_DIM ** -0.5 * 1.4426950408889634
SUM_ROWS = 16
V_ROWS = HEAD_DIM + SUM_ROWS

F32 = jnp.float32
BF16 = jnp.bfloat16

V7X_VMEM_LIMIT_BYTES = 56 * 1024 * 1024
ROW_BLOCK = 512
Q_BLOCK = 512
KV_CHUNK = ROW_BLOCK
LANES = 128
HALO = 16
CONV_ROWS = 32
FF_CHUNK = 256
FFN_PIECES = 4
NORM_ROWS = 32
MAX_SLAB = 64
SCORES_AHEAD = 2
CHUNK_UNROLL = 8
MAX_SAFE_BOUND = 60.0
BOUND_SLACK = 1.0 + 2.0 ** -6


def _params(*sem):
    return pltpu.CompilerParams(dimension_semantics=sem, vmem_limit_bytes=V7X_VMEM_LIMIT_BYTES)


def _ln_noaffine(x):
    mu = jnp.mean(x, axis=-1, keepdims=True)
    xc = x - mu
    var = jnp.mean(xc * xc, axis=-1, keepdims=True)
    return xc * lax.rsqrt(var + EPS)


def _adaln_kernel(c_ref, w_ref, b_ref, o_ref):
    c = c_ref[...]
    c_act = c * jax.nn.sigmoid(c)
    w = w_ref[...]
    c_hi, w_hi = c_act.astype(BF16), w.astype(BF16)
    c_lo = (c_act - c_hi.astype(F32)).astype(BF16)
    w_lo = (w - w_hi.astype(F32)).astype(BF16)
    o_ref[...] = (jnp.dot(c_hi, w_hi, preferred_element_type=F32) +
                  jnp.dot(c_hi, w_lo, preferred_element_type=F32) +
                  jnp.dot(c_lo, w_hi, preferred_element_type=F32)) + b_ref[...]


def _adaln(c_pad, w, b):
    rows, d = c_pad.shape
    n = w.shape[1]
    return pl.pallas_call(
        _adaln_kernel,
        out_shape=jax.ShapeDtypeStruct((rows, n), F32),
        grid=(n // d,),
        in_specs=[pl.BlockSpec((rows, d), lambda j: (0, 0)),
                  pl.BlockSpec((d, d), lambda j: (0, j)),
                  pl.BlockSpec((1, d), lambda j: (0, j))],
        out_specs=pl.BlockSpec((rows, d), lambda j: (0, j)),
        compiler_params=_params("arbitrary"),
        name="adaln",
    )(c_pad, w, b)


def _rope_t(x, tab):
    f = ROPE_FREQS
    cr, sr, cc, sc = tab[0:f], tab[f:2 * f], tab[2 * f:3 * f], tab[3 * f:4 * f]
    x1r, x2r, x1c, x2c = x[0:f], x[f:2 * f], x[2 * f:3 * f], x[3 * f:4 * f]
    return jnp.concatenate([x1r * cr - x2r * sr, x2r * cr + x1r * sr,
                            x1c * cc - x2c * sc, x2c * cc + x1c * sc], axis=0)


def _norm_rope_t(xt, gain, tab):
    ms = jnp.mean(xt * xt, axis=0, keepdims=True)
    return _rope_t(xt * lax.rsqrt(ms + EPS) * gain, tab)


def _inproj_kernel(x_ref, mod_ref, w_ref, gains_ref, tab_ref, qt_ref, k_ref, vt_ref, h_ref, kn_ref,
                   u_ref, proj_ref):
    shift, scale = mod_ref[0:1, :], mod_ref[1:2, :]
    t = x_ref.shape[0]
    half = t // 2
    qkv_w = ATTN_WIDTH + 2 * KV_WIDTH
    conv_ch = (w_ref.shape[1] - qkv_w) // 2

    for r0 in (0, half):
        for r in range(r0, r0 + half, NORM_ROWS):
            u = _ln_noaffine(x_ref[r:r + NORM_ROWS, :]) * (1.0 + scale) + shift
            u_ref[r:r + NORM_ROWS, :] = u.astype(BF16)
        proj_ref[r0:r0 + half, :] = jnp.dot(u_ref[r0:r0 + half, :], w_ref[...],
                                            preferred_element_type=F32)

    for r0 in (0, half):
        cols = slice(r0, r0 + half)
        qkv_t = proj_ref[r0:r0 + half, 0:qkv_w].T
        tab = tab_ref[:, cols]
        gq, gk = gains_ref[0, :, 0:half], gains_ref[1, :, 0:half]
        for h in range(N_HEADS):
            r = h * HEAD_DIM
            qh = _norm_rope_t(qkv_t[r:r + HEAD_DIM], gq, tab) * Q_SCALE
            qt_ref[r:r + HEAD_DIM, cols] = qh.astype(BF16)
        k_rot = []
        for h in range(N_KV_HEADS):
            r = ATTN_WIDTH + h * HEAD_DIM
            kh = _norm_rope_t(qkv_t[r:r + HEAD_DIM], gk, tab).astype(BF16)
            k_rot.append(kh)
            khf = kh.astype(F32)
            kn_ref[h:h + 1, cols] = jnp.sum(khf * khf, axis=0, keepdims=True)
        k_ref[cols, :] = jnp.concatenate(k_rot, axis=0).astype(F32).T.astype(BF16)
        for h in range(N_KV_HEADS):
            r = ATTN_WIDTH + KV_WIDTH + h * HEAD_DIM
            vt_ref[h, 0:HEAD_DIM, cols] = qkv_t[r:r + HEAD_DIM].astype(BF16)
            vt_ref[h, HEAD_DIM:, cols] = jnp.ones((SUM_ROWS, half), BF16)

        a = proj_ref[r0:r0 + half, qkv_w:qkv_w + conv_ch]
        g = proj_ref[r0:r0 + half, qkv_w + conv_ch:]
        h_ref[r0:r0 + half, :] = a * jax.nn.sigmoid(g)


def _inproj(x, mod, w_bf, gains, tab):
    b, s, d = x.shape
    t = ROW_BLOCK
    n_cols = w_bf.shape[1]
    conv_ch = (n_cols - ATTN_WIDTH - 2 * KV_WIDTH) // 2
    return pl.pallas_call(
        _inproj_kernel,
        out_shape=(jax.ShapeDtypeStruct((b, ATTN_WIDTH, s), BF16),
                   jax.ShapeDtypeStruct((b, s, KV_WIDTH), BF16),
                   jax.ShapeDtypeStruct((b, s // t, N_KV_HEADS, V_ROWS, t), BF16),
                   jax.ShapeDtypeStruct((b, s, conv_ch), F32),
                   jax.ShapeDtypeStruct((b, N_KV_HEADS, s), F32)),
        grid=(b, s // t),
        in_specs=[pl.BlockSpec((None, t, d), lambda bi, i: (bi, i, 0)),
                  pl.BlockSpec((None, 6, d), lambda bi, i: (bi, 0, 0)),
                  pl.BlockSpec((d, n_cols), lambda bi, i: (0, 0)),
                  pl.BlockSpec((2, HEAD_DIM, t), lambda bi, i: (0, 0, 0)),
                  pl.BlockSpec((HEAD_DIM, t), lambda bi, i: (0, i))],
        out_specs=(pl.BlockSpec((None, ATTN_WIDTH, t), lambda bi, i: (bi, 0, i)),
                   pl.BlockSpec((None, t, KV_WIDTH), lambda bi, i: (bi, i, 0)),
                   pl.BlockSpec((None, None, N_KV_HEADS, V_ROWS, t), lambda bi, i: (bi, i, 0, 0, 0)),
                   pl.BlockSpec((None, t, conv_ch), lambda bi, i: (bi, i, 0)),
                   pl.BlockSpec((None, N_KV_HEADS, t), lambda bi, i: (bi, 0, i))),
        scratch_shapes=[pltpu.VMEM((t, d), BF16),
                        pltpu.VMEM((t, n_cols), F32)],
        compiler_params=_params("parallel", "parallel"),
        name="inproj",
    )(x, mod, w_bf, gains, tab)


def _attn_kernel(qt_ref, k_ref, vt_ref, kn_ref, g_ref, o_ref, qp_ref, s_ref, p_ref, m_ref, acc_ref):
    grp = pl.program_id(1)
    n_chunks, tk = k_ref.shape[0], k_ref.shape[1]
    tq = qt_ref.shape[1]

    first = grp == 0
    k_norm2_max = jnp.max(kn_ref[...], axis=1, keepdims=True)
    bounds = []
    for h in range(GROUPS):
        qh = qt_ref[h * HEAD_DIM:(h + 1) * HEAD_DIM, :]
        z = jnp.zeros_like(qh)
        qp_ref[h, 0:HEAD_DIM, :] = jnp.where(first, qh, z)
        qp_ref[h, HEAD_DIM:2 * HEAD_DIM, :] = jnp.where(first, z, qh)
        qf = qh.astype(F32)
        q_norm2 = jnp.sum(qf * qf, axis=0, keepdims=True)
        bounds.append(jnp.sqrt(q_norm2 * k_norm2_max) * BOUND_SLACK)
    acc_ref[...] = jnp.zeros(acc_ref.shape, F32)
    bound_max = jnp.max(jnp.concatenate(bounds, axis=0))

    def scores(c, h):
        return jnp.dot(k_ref[c], qp_ref[h], preferred_element_type=F32)

    @pl.when(bound_max <= MAX_SAFE_BOUND)
    def _():
        for h in range(GROUPS):
            m_ref[h] = bounds[h]

        def probs(c, h, slot):
            p_ref[slot] = jnp.exp2(scores(c, h) - m_ref[h]).astype(BF16)

        probs(0, 0, 0)

        def chunk(c, carry):
            vc = vt_ref[c]
            c_next = jnp.minimum(c + 1, n_chunks - 1)
            for h in range(GROUPS):
                if h + 1 < GROUPS:
                    probs(c, h + 1, (h + 1) % 2)
                else:
                    probs(c_next, 0, 0)
                acc_ref[h] += jnp.dot(vc, p_ref[h % 2], preferred_element_type=F32)
            return carry

        lax.fori_loop(0, n_chunks, chunk, 0, unroll=CHUNK_UNROLL)

    @pl.when(jnp.logical_not(bound_max <= MAX_SAFE_BOUND))
    def _():
        m_ref[...] = jnp.full(m_ref.shape, -jnp.inf, F32)
        for h in range(SCORES_AHEAD):
            s_ref[h] = scores(0, h)

        def chunk(c, carry):
            vc = vt_ref[c]
            c_next = jnp.minimum(c + 1, n_chunks - 1)
            for h in range(GROUPS):
                ahead = h + SCORES_AHEAD
                if ahead < GROUPS:
                    s_ref[ahead] = scores(c, ahead)
                else:
                    s_ref[ahead - GROUPS] = scores(c_next, ahead - GROUPS)
                s = s_ref[h]
                m_old = m_ref[h]
                slab_max = jnp.max(s.reshape(tk // MAX_SLAB, MAX_SLAB, tq), axis=0)
                m_new = jnp.maximum(m_old, jnp.max(slab_max, axis=0, keepdims=True))
                p = jnp.exp2(s - m_new)
                alpha = jnp.exp2(m_old - m_new)
                pv = jnp.dot(vc, p.astype(BF16), preferred_element_type=F32)
                acc_ref[h] = alpha * acc_ref[h] + pv
                m_ref[h] = m_new
            return carry

        lax.fori_loop(0, n_chunks, chunk, 0, unroll=2)

    outs = []
    for h in range(GROUPS):
        o = acc_ref[h, 0:HEAD_DIM, :] * (1.0 / acc_ref[h, HEAD_DIM:HEAD_DIM + 1, :])
        ms = jnp.mean(o * o, axis=0, keepdims=True)
        outs.append(o * lax.rsqrt(ms + EPS) * g_ref[h * HEAD_DIM:(h + 1) * HEAD_DIM, :])
    o_ref[...] = jnp.concatenate(outs, axis=0).T.astype(BF16)


def _attention(qt, k4, vt4, kn4, gain_t):
    b, _, s = qt.shape
    n_chunks, tk = k4.shape[1], k4.shape[2]
    tq = Q_BLOCK
    gw = GROUPS * HEAD_DIM
    return pl.pallas_call(
        _attn_kernel,
        out_shape=jax.ShapeDtypeStruct((b, s, ATTN_WIDTH), BF16),
        grid=(b, N_KV_HEADS, s // tq),
        in_specs=[pl.BlockSpec((None, gw, tq), lambda bi, g, qi: (bi, g, qi)),
                  pl.BlockSpec((None, n_chunks, tk, KV_WIDTH), lambda bi, g, qi: (bi, 0, 0, 0)),
                  pl.BlockSpec((None, n_chunks, None, V_ROWS, tk), lambda bi, g, qi: (bi, 0, g, 0, 0)),
                  pl.BlockSpec((None, None, 1, s), lambda bi, g, qi: (bi, g, 0, 0)),
                  pl.BlockSpec((None, gw, tq), lambda bi, g, qi: (g, 0, 0))],
        out_specs=pl.BlockSpec((None, tq, gw), lambda bi, g, qi: (bi, qi, g)),
        scratch_shapes=[pltpu.VMEM((GROUPS, 2 * HEAD_DIM, tq), BF16),
                        pltpu.VMEM((GROUPS, tk, tq), F32),
                        pltpu.VMEM((2, tk, tq), BF16),
                        pltpu.VMEM((GROUPS, 1, tq), F32),
                        pltpu.VMEM((GROUPS, V_ROWS, tq), F32)],
        compiler_params=_params("parallel", "parallel", "parallel"),
        name="attn",
    )(qt, k4, vt4, kn4, gain_t)


def _group_mean(sq, gmat):
    hi = sq.astype(BF16)
    lo = (sq - hi.astype(F32)).astype(BF16)
    return (jnp.dot(hi, gmat, preferred_element_type=F32) +
            jnp.dot(lo, gmat, preferred_element_type=F32))


def _convmod_kernel(prev_ref, cur_ref, next_ref, dw_ref, vec_ref, pw_ref, gmat_ref, o_ref, ext_ref, z_ref):
    i = pl.program_id(1)
    last = pl.num_programs(1) - 1
    t = cur_ref.shape[0]
    n_slab = cur_ref.shape[1] // LANES
    prev = jnp.where(i > 0, prev_ref[...], 0.0)
    nxt = jnp.where(i < last, next_ref[...], 0.0)
    for sl in range(n_slab):
        lanes = slice(sl * LANES, (sl + 1) * LANES)
        ext_ref[sl, 0:HALO, :] = prev[:, lanes]
        ext_ref[sl, HALO:HALO + t, :] = cur_ref[:, lanes]
        ext_ref[sl, HALO + t:, :] = nxt[:, lanes]

    dw_b, ln_g, ln_b, out_g = vec_ref[0:1, :], vec_ref[1:2, :], vec_ref[2:3, :], vec_ref[3:4, :]
    base = HALO - CONV_PAD
    rows_par = CONV_ROWS // 2
    for r in range(0, t, CONV_ROWS):
        for parity in range(2):
            slabs = []
            for sl in range(n_slab):
                lanes = slice(sl * LANES, (sl + 1) * LANES)
                acc = jnp.broadcast_to(dw_b[:, lanes], (rows_par, LANES))
                for j in range(CONV_WIDTH):
                    window = ext_ref[sl, pl.ds(r + parity + j + base, rows_par, stride=2), :]
                    acc = acc + dw_ref[j:j + 1, lanes] * window
                slabs.append(acc)
            y = _ln_noaffine(jnp.concatenate(slabs, axis=1)) * ln_g + ln_b
            z = y * jax.nn.sigmoid(y)
            for sl in range(n_slab):
                z_ref[sl, pl.ds(r + parity, rows_par, stride=2), :] = z[:, sl * LANES:(sl + 1) * LANES]

    z_rows = jnp.concatenate([z_ref[sl] for sl in range(n_slab)], axis=1).astype(BF16)
    pw = jnp.dot(z_rows, pw_ref[...], preferred_element_type=F32)
    ms = _group_mean(pw * pw, gmat_ref[...])
    o_ref[...] = (pw * lax.rsqrt(ms + EPS) * out_g).astype(BF16)


def _convmod(h, dw_w, vecs, pw_bf, gmat):
    b, s, ch = h.shape
    t = ROW_BLOCK
    hb = t // HALO
    n_halo = s // HALO
    return pl.pallas_call(
        _convmod_kernel,
        out_shape=jax.ShapeDtypeStruct((b, s, ch), BF16),
        grid=(b, s // t),
        in_specs=[pl.BlockSpec((None, HALO, ch), lambda bi, i: (bi, jnp.maximum(i * hb - 1, 0), 0)),
                  pl.BlockSpec((None, t, ch), lambda bi, i: (bi, i, 0)),
                  pl.BlockSpec((None, HALO, ch), lambda bi, i: (bi, jnp.minimum((i + 1) * hb, n_halo - 1), 0)),
                  pl.BlockSpec(dw_w.shape, lambda bi, i: (0, 0)),
                  pl.BlockSpec(vecs.shape, lambda bi, i: (0, 0)),
                  pl.BlockSpec(pw_bf.shape, lambda bi, i: (0, 0)),
                  pl.BlockSpec(gmat.shape, lambda bi, i: (0, 0))],
        out_specs=pl.BlockSpec((None, t, ch), lambda bi, i: (bi, i, 0)),
        scratch_shapes=[pltpu.VMEM((ch // LANES, t + 2 * HALO, LANES), F32),
                        pltpu.VMEM((ch // LANES, t, LANES), F32)],
        compiler_params=_params("parallel", "parallel"),
        name="convmod",
    )(h, h, h, dw_w, vecs, pw_bf, gmat)


def _outproj_kernel(alpha, x_ref, a_ref, c_ref, mod_ref, wa_ref, wc_ref, ln_ref, x1_ref, u2_ref, mix_ref):
    t = x_ref.shape[0]
    half = t // 2
    for r in (0, half):
        mix_ref[r:r + half, :] = (
            jnp.dot(a_ref[r:r + half, :], wa_ref[...], preferred_element_type=F32) +
            jnp.dot(c_ref[r:r + half, :], wc_ref[...], preferred_element_type=F32))
    gate1, shift2, scale2 = mod_ref[2:3, :], mod_ref[3:4, :], mod_ref[4:5, :]
    for r in range(0, t, NORM_ROWS):
        y = alpha * x_ref[r:r + NORM_ROWS, :] + gate1 * mix_ref[r:r + NORM_ROWS, :]
        x1 = _ln_noaffine(y) * ln_ref[0:1, :] + ln_ref[1:2, :]
        x1_ref[r:r + NORM_ROWS, :] = x1
        u2_ref[r:r + NORM_ROWS, :] = (_ln_noaffine(x1) * (1.0 + scale2) + shift2).astype(BF16)


def _outproj(alpha, x, attn, hc, mod, wo_a, wo_c, ln1):
    b, s, d = x.shape
    t = ROW_BLOCK
    aw, cw = attn.shape[2], hc.shape[2]
    return pl.pallas_call(
        functools.partial(_outproj_kernel, alpha),
        out_shape=(jax.ShapeDtypeStruct((b, s, d), F32),
                   jax.ShapeDtypeStruct((b, s, d), BF16)),
        grid=(b, s // t),
        in_specs=[pl.BlockSpec((None, t, d), lambda bi, i: (bi, i, 0)),
                  pl.BlockSpec((None, t, aw), lambda bi, i: (bi, i, 0)),
                  pl.BlockSpec((None, t, cw), lambda bi, i: (bi, i, 0)),
                  pl.BlockSpec((None, 6, d), lambda bi, i: (bi, 0, 0)),
                  pl.BlockSpec((aw, d), lambda bi, i: (0, 0)),
                  pl.BlockSpec((cw, d), lambda bi, i: (0, 0)),
                  pl.BlockSpec((2, d), lambda bi, i: (0, 0))],
        out_specs=(pl.BlockSpec((None, t, d), lambda bi, i: (bi, i, 0)),
                   pl.BlockSpec((None, t, d), lambda bi, i: (bi, i, 0))),
        scratch_shapes=[pltpu.VMEM((t, d), F32)],
        compiler_params=_params("parallel", "parallel"),
        name="outproj",
    )(x, attn, hc, mod, wo_a, wo_c, ln1)


def _ffn_kernel(alpha, prev_ref, cur_ref, next_ref, x1_ref, mod_ref, wup_ref, dw_ref, wdn_ref, ln_ref,
                o_ref, ext_ref, hv_ref, hg_ref, act_ref, acc_ref):
    i = pl.program_id(1)
    last = pl.num_programs(1) - 1
    t = cur_ref.shape[0]
    d_ff = wdn_ref.shape[0]
    ext_ref[0:HALO, :] = jnp.where(i > 0, prev_ref[...], jnp.zeros_like(prev_ref[...]))
    ext_ref[HALO:HALO + t, :] = cur_ref[...]
    ext_ref[HALO + t:, :] = jnp.where(i < last, next_ref[...], jnp.zeros_like(next_ref[...]))
    ext = ext_ref[...]

    piece = t // FFN_PIECES
    rows_par = piece // 2
    n_slab = FF_CHUNK // LANES

    def conv3(h_ref, slot, sl, col, first):
        w = dw_ref[:, col + sl * LANES:col + (sl + 1) * LANES]
        taps = [h_ref[slot, sl, pl.ds(first + k, rows_par, stride=2), :] for k in range(FFN_CONV_WIDTH)]
        return w[0:1] * taps[0] + w[1:2] * taps[1] + w[2:3] * taps[2] + w[3:4]

    def up_into(h_ref, c, slot):
        r = jnp.dot(ext, wup_ref[:, c:c + FF_CHUNK], preferred_element_type=F32)
        for sl in range(n_slab):
            h_ref[slot, sl] = r[:, sl * LANES:(sl + 1) * LANES]

    def up_val(c, slot):
        up_into(hv_ref, c, slot)

    def up_gate(c, slot):
        up_into(hg_ref, d_ff + c, slot)

    issue_before = {0: up_val, FFN_PIECES // 2: up_gate}
    up_val(0, 0)
    up_gate(0, 0)
    acc_ref[...] = jnp.zeros(acc_ref.shape, F32)
    for n, c in enumerate(range(0, d_ff, FF_CHUNK)):
        slot = n % 2
        for pi in range(FFN_PIECES):
            if c + FF_CHUNK < d_ff and pi in issue_before:
                issue_before[pi](c + FF_CHUNK, 1 - slot)
            r0 = pi * piece
            for sl in range(n_slab):
                for parity in range(2):
                    first = HALO + r0 + parity - 1
                    val = conv3(hv_ref, slot, sl, c, first)
                    gt = conv3(hg_ref, slot, sl, d_ff + c, first)
                    act = 0.5 * gt * (1.0 + lax.erf(gt * (2.0 ** -0.5))) * val
                    act_ref[sl, pl.ds(r0 + parity, rows_par, stride=2), :] = act
            act_rows = jnp.concatenate([act_ref[sl, r0:r0 + piece, :] for sl in range(n_slab)], axis=1)
            acc_ref[r0:r0 + piece, :] += jnp.dot(act_rows.astype(BF16), wdn_ref[c:c + FF_CHUNK, :],
                                                 preferred_element_type=F32)

    gate2 = mod_ref[5:6, :]
    for r in range(0, t, NORM_ROWS):
        y = alpha * x1_ref[r:r + NORM_ROWS, :] + gate2 * acc_ref[r:r + NORM_ROWS, :]
        o_ref[r:r + NORM_ROWS, :] = _ln_noaffine(y) * ln_ref[0:1, :] + ln_ref[1:2, :]


def _ffn(alpha, u2, x1, mod, wup_bf, dw4, wdn_bf, ln2):
    b, s, d = x1.shape
    t = ROW_BLOCK
    hb = t // HALO
    n_halo = s // HALO
    resident = dict(pipeline_mode=pl.Buffered(1))
    return pl.pallas_call(
        functools.partial(_ffn_kernel, alpha),
        out_shape=jax.ShapeDtypeStruct((b, s, d), F32),
        grid=(b, s // t),
        in_specs=[pl.BlockSpec((None, HALO, d), lambda bi, i: (bi, jnp.maximum(i * hb - 1, 0), 0)),
                  pl.BlockSpec((None, t, d), lambda bi, i: (bi, i, 0)),
                  pl.BlockSpec((None, HALO, d), lambda bi, i: (bi, jnp.minimum((i + 1) * hb, n_halo - 1), 0)),
                  pl.BlockSpec((None, t, d), lambda bi, i: (bi, i, 0)),
                  pl.BlockSpec((None, 6, d), lambda bi, i: (bi, 0, 0)),
                  pl.BlockSpec(wup_bf.shape, lambda bi, i: (0, 0), **resident),
                  pl.BlockSpec(dw4.shape, lambda bi, i: (0, 0)),
                  pl.BlockSpec(wdn_bf.shape, lambda bi, i: (0, 0), **resident),
                  pl.BlockSpec((2, d), lambda bi, i: (0, 0))],
        out_specs=pl.BlockSpec((None, t, d), lambda bi, i: (bi, i, 0)),
        scratch_shapes=[pltpu.VMEM((t + 2 * HALO, d), BF16),
                        pltpu.VMEM((2, FF_CHUNK // LANES, t + 2 * HALO, LANES), F32),
                        pltpu.VMEM((2, FF_CHUNK // LANES, t + 2 * HALO, LANES), F32),
                        pltpu.VMEM((FF_CHUNK // LANES, t, LANES), F32),
                        pltpu.VMEM((t, d), F32)],
        compiler_params=_params("parallel", "parallel"),
        name="ffn",
    )(u2, u2, u2, x1, mod, wup_bf, dw4, wdn_bf, ln2)


def _rope_table_t(seq_len):
    pos = jnp.arange(seq_len, dtype=jnp.int32)
    rows = (pos // GRID_W).astype(F32)
    cols = (pos % GRID_W).astype(F32)
    inv_freq = ROPE_THETA ** (-jnp.arange(ROPE_FREQS, dtype=F32) / ROPE_FREQS)
    ang_r = inv_freq[:, None] * rows[None, :]
    ang_c = inv_freq[:, None] * cols[None, :]
    return jnp.concatenate([jnp.cos(ang_r), jnp.sin(ang_r), jnp.cos(ang_c), jnp.sin(ang_c)], axis=0)


def kernel(x, c, w_ada, b_ada, w_in, q_norm_g, k_norm_g, conv_dw_w, conv_dw_b, conv_ln_g, conv_ln_b,
           w_conv_pw2, attn_out_g, conv_out_g, w_o, ln1_g, ln1_b, w_up, ffn_dw_w, ffn_dw_b, w_down,
           ln2_g, ln2_b):
    b, s, d = x.shape
    depth = w_ada.shape[0]
    alpha = (2.0 * depth) ** 0.25
    conv_ch = w_conv_pw2.shape[1]
    assert s % ROW_BLOCK == 0 and s % Q_BLOCK == 0 and s % GRID_W == 0
    assert w_down.shape[1] % FF_CHUNK == 0 and b <= 8

    tab = _rope_table_t(s)
    c_pad = jnp.zeros((8, d), F32).at[:b].set(c)
    gidx = jnp.arange(conv_ch) // CONV_GROUP_DIM
    gmat = jnp.where(gidx[:, None] == gidx[None, :], 1.0 / CONV_GROUP_DIM, 0.0).astype(BF16)

    for l in range(depth):
        mod = _adaln(c_pad, w_ada[l], b_ada[l][None, :])[:b].reshape(b, 6, d)

        gains = jnp.stack([jnp.broadcast_to(q_norm_g[l][:, None], (HEAD_DIM, ROW_BLOCK)),
                           jnp.broadcast_to(k_norm_g[l][:, None], (HEAD_DIM, ROW_BLOCK))])
        qt, k, vt4, h, kn = _inproj(x, mod, w_in[l].astype(BF16), gains, tab)

        k4 = k.reshape(b, s // KV_CHUNK, KV_CHUNK, KV_WIDTH)
        gain_t = jnp.broadcast_to(attn_out_g[l].reshape(N_KV_HEADS, GROUPS * HEAD_DIM, 1),
                                  (N_KV_HEADS, GROUPS * HEAD_DIM, Q_BLOCK))
        attn = _attention(qt, k4, vt4, kn.reshape(b, N_KV_HEADS, 1, s), gain_t)

        dw_w = jnp.zeros((CONV_WIDTH + 1, conv_ch), F32).at[:CONV_WIDTH].set(conv_dw_w[l])
        vecs = jnp.stack([conv_dw_b[l], conv_ln_g[l], conv_ln_b[l], conv_out_g[l].reshape(-1)])
        hc = _convmod(h, dw_w, vecs, w_conv_pw2[l].astype(BF16), gmat)

        wo_bf = w_o[l].astype(BF16)
        x1, u2 = _outproj(alpha, x, attn, hc, mod, wo_bf[:ATTN_WIDTH], wo_bf[ATTN_WIDTH:],
                          jnp.stack([ln1_g[l], ln1_b[l]]))

        dw4 = jnp.concatenate([ffn_dw_w[l], ffn_dw_b[l][None, :]], axis=0)
        x = _ffn(alpha, u2, x1, mod, w_up[l].astype(BF16), dw4, w_down[l].astype(BF16),
                 jnp.stack([ln2_g[l], ln2_b[l]]))
    return x
```

```python
import functools

import jax
import jax.numpy as jnp
from jax import lax
from jax.experimental import pallas as pl
from jax.experimental.pallas import tpu as pltpu

GRID_W = 64
HEAD_DIM = 64
N_HEADS = 8
N_KV_HEADS = 2
GROUPS = N_HEADS // N_KV_HEADS
ATTN_WIDTH = N_HEADS * HEAD_DIM
KV_WIDTH = N_KV_HEADS * HEAD_DIM
CONV_GROUP_DIM = 64
CONV_WIDTH = 31
CONV_PAD = (CONV_WIDTH - 1) // 2
FFN_CONV_WIDTH = 3
ROPE_THETA = 10000.0
ROPE_FREQS = HEAD_DIM // 4
EPS = 1e-6
Q_SCALE = HEAD_DIM ** -0.5 * 1.4426950408889634
SUM_ROWS = 16
V_ROWS = HEAD_DIM + SUM_ROWS

F32 = jnp.float32
BF16 = jnp.bfloat16

V7X_VMEM_LIMIT_BYTES = 56 * 1024 * 1024
ROW_BLOCK = 512
Q_BLOCK = 512
KV_CHUNK = ROW_BLOCK
LANES = 128
HALO = 16
CONV_ROWS = 32
FF_CHUNK = 256
FFN_PIECES = 2
NORM_ROWS = 32
MAX_SLAB = 64
SCORES_AHEAD = 2
CHUNK_UNROLL = 8
MAX_SAFE_BOUND = 60.0
BOUND_SLACK = 1.0 + 2.0 ** -6


def _params(*sem):
    return pltpu.CompilerParams(dimension_semantics=sem, vmem_limit_bytes=V7X_VMEM_LIMIT_BYTES)


def _ln_noaffine(x):
    mu = jnp.mean(x, axis=-1, keepdims=True)
    xc = x - mu
    var = jnp.mean(xc * xc, axis=-1, keepdims=True)
    return xc * lax.rsqrt(var + EPS)


def _adaln_kernel(c_ref, w_ref, b_ref, o_ref):
    c = c_ref[...]
    c_act = c * jax.nn.sigmoid(c)
    w = w_ref[...]
    c_hi, w_hi = c_act.astype(BF16), w.astype(BF16)
    c_lo = (c_act - c_hi.astype(F32)).astype(BF16)
    w_lo = (w - w_hi.astype(F32)).astype(BF16)
    o_ref[...] = (jnp.dot(c_hi, w_hi, preferred_element_type=F32) +
                  jnp.dot(c_hi, w_lo, preferred_element_type=F32) +
                  jnp.dot(c_lo, w_hi, preferred_element_type=F32)) + b_ref[...]


def _adaln(c_pad, w, b):
    rows, d = c_pad.shape
    n = w.shape[1]
    cols = 2 * d
    return pl.pallas_call(
        _adaln_kernel,
        out_shape=jax.ShapeDtypeStruct((rows, n), F32),
        grid=(n // cols,),
        in_specs=[pl.BlockSpec((rows, d), lambda j: (0, 0)),
                  pl.BlockSpec((d, cols), lambda j: (0, j)),
                  pl.BlockSpec((1, cols), lambda j: (0, j))],
        out_specs=pl.BlockSpec((rows, cols), lambda j: (0, j)),
        compiler_params=_params("arbitrary"),
        name="adaln",
    )(c_pad, w, b)


def _rope_t(x, tab):
    f = ROPE_FREQS
    cr, sr, cc, sc = tab[0:f], tab[f:2 * f], tab[2 * f:3 * f], tab[3 * f:4 * f]
    x1r, x2r, x1c, x2c = x[0:f], x[f:2 * f], x[2 * f:3 * f], x[3 * f:4 * f]
    return jnp.concatenate([x1r * cr - x2r * sr, x2r * cr + x1r * sr,
                            x1c * cc - x2c * sc, x2c * cc + x1c * sc], axis=0)


def _norm_rope_t(xt, gain, tab):
    ms = jnp.mean(xt * xt, axis=0, keepdims=True)
    return _rope_t(xt * lax.rsqrt(ms + EPS) * gain, tab)


def _inproj_kernel(x_ref, mod_ref, w_ref, gains_ref, tab_ref, qt_ref, k_ref, vt_ref, h_ref, kn_ref,
                   u_ref, proj_ref):
    shift, scale = mod_ref[0:1, :], mod_ref[1:2, :]
    t = x_ref.shape[0]
    half = t // 2
    qkv_w = ATTN_WIDTH + 2 * KV_WIDTH
    conv_ch = (w_ref.shape[1] - qkv_w) // 2

    for r0 in (0, half):
        for r in range(r0, r0 + half, NORM_ROWS):
            u = _ln_noaffine(x_ref[r:r + NORM_ROWS, :]) * (1.0 + scale) + shift
            u_ref[r:r + NORM_ROWS, :] = u.astype(BF16)
        proj_ref[r0:r0 + half, :] = jnp.dot(u_ref[r0:r0 + half, :], w_ref[...],
                                            preferred_element_type=F32)

    for r0 in (0, half):
        cols = slice(r0, r0 + half)
        qkv_t = proj_ref[r0:r0 + half, 0:qkv_w].T
        tab = tab_ref[:, cols]
        gq, gk = gains_ref[0, :, 0:half], gains_ref[1, :, 0:half]
        for h in range(N_HEADS):
            r = h * HEAD_DIM
            qh = _norm_rope_t(qkv_t[r:r + HEAD_DIM], gq, tab) * Q_SCALE
            qt_ref[r:r + HEAD_DIM, cols] = qh.astype(BF16)
        k_rot = []
        for h in range(N_KV_HEADS):
            r = ATTN_WIDTH + h * HEAD_DIM
            kh = _norm_rope_t(qkv_t[r:r + HEAD_DIM], gk, tab).astype(BF16)
            k_rot.append(kh)
            khf = kh.astype(F32)
            kn_ref[h:h + 1, cols] = jnp.sum(khf * khf, axis=0, keepdims=True)
        k_ref[cols, :] = jnp.concatenate(k_rot, axis=0).astype(F32).T.astype(BF16)
        for h in range(N_KV_HEADS):
            r = ATTN_WIDTH + KV_WIDTH + h * HEAD_DIM
            vt_ref[h, 0:HEAD_DIM, cols] = qkv_t[r:r + HEAD_DIM].astype(BF16)
            vt_ref[h, HEAD_DIM:, cols] = jnp.ones((SUM_ROWS, half), BF16)

        a = proj_ref[r0:r0 + half, qkv_w:qkv_w + conv_ch]
        g = proj_ref[r0:r0 + half, qkv_w + conv_ch:]
        h_ref[r0:r0 + half, :] = a * jax.nn.sigmoid(g)


def _inproj(x, mod, w_bf, gains, tab):
    b, s, d = x.shape
    t = ROW_BLOCK
    n_cols = w_bf.shape[1]
    conv_ch = (n_cols - ATTN_WIDTH - 2 * KV_WIDTH) // 2
    return pl.pallas_call(
        _inproj_kernel,
        out_shape=(jax.ShapeDtypeStruct((b, ATTN_WIDTH, s), BF16),
                   jax.ShapeDtypeStruct((b, s, KV_WIDTH), BF16),
                   jax.ShapeDtypeStruct((b, s // t, N_KV_HEADS, V_ROWS, t), BF16),
                   jax.ShapeDtypeStruct((b, s, conv_ch), F32),
                   jax.ShapeDtypeStruct((b, N_KV_HEADS, s), F32)),
        grid=(b, s // t),
        in_specs=[pl.BlockSpec((None, t, d), lambda bi, i: (bi, i, 0)),
                  pl.BlockSpec((None, 6, d), lambda bi, i: (bi, 0, 0)),
                  pl.BlockSpec((d, n_cols), lambda bi, i: (0, 0)),
                  pl.BlockSpec((2, HEAD_DIM, t), lambda bi, i: (0, 0, 0)),
                  pl.BlockSpec((HEAD_DIM, t), lambda bi, i: (0, i))],
        out_specs=(pl.BlockSpec((None, ATTN_WIDTH, t), lambda bi, i: (bi, 0, i)),
                   pl.BlockSpec((None, t, KV_WIDTH), lambda bi, i: (bi, i, 0)),
                   pl.BlockSpec((None, None, N_KV_HEADS, V_ROWS, t), lambda bi, i: (bi, i, 0, 0, 0)),
                   pl.BlockSpec((None, t, conv_ch), lambda bi, i: (bi, i, 0)),
                   pl.BlockSpec((None, N_KV_HEADS, t), lambda bi, i: (bi, 0, i))),
        scratch_shapes=[pltpu.VMEM((t, d), BF16),
                        pltpu.VMEM((t, n_cols), F32)],
        compiler_params=_params("parallel", "parallel"),
        name="inproj",
    )(x, mod, w_bf, gains, tab)


def _attn_kernel(qt_ref, k_ref, vt_ref, kn_ref, g_ref, o_ref, qp_ref, s_ref, p_ref, m_ref, acc_ref):
    grp = pl.program_id(1)
    n_chunks, tk = k_ref.shape[0], k_ref.shape[1]
    tq = qt_ref.shape[1]

    first = grp == 0
    k_norm2_max = jnp.max(kn_ref[...], axis=1, keepdims=True)
    bounds = []
    for h in range(GROUPS):
        qh = qt_ref[h * HEAD_DIM:(h + 1) * HEAD_DIM, :]
        z = jnp.zeros_like(qh)
        qp_ref[h, 0:HEAD_DIM, :] = jnp.where(first, qh, z)
        qp_ref[h, HEAD_DIM:2 * HEAD_DIM, :] = jnp.where(first, z, qh)
        qf = qh.astype(F32)
        q_norm2 = jnp.sum(qf * qf, axis=0, keepdims=True)
        bounds.append(jnp.sqrt(q_norm2 * k_norm2_max) * BOUND_SLACK)
    acc_ref[...] = jnp.zeros(acc_ref.shape, F32)
    bound_max = jnp.max(jnp.concatenate(bounds, axis=0))

    def scores(c, h):
        return jnp.dot(k_ref[c], qp_ref[h], preferred_element_type=F32)

    @pl.when(bound_max <= MAX_SAFE_BOUND)
    def _():
        for h in range(GROUPS):
            m_ref[h] = bounds[h]

        def probs(c, h, slot):
            p_ref[slot] = jnp.exp2(scores(c, h) - m_ref[h]).astype(BF16)

        probs(0, 0, 0)

        def chunk(c, carry):
            vc = vt_ref[c]
            c_next = jnp.minimum(c + 1, n_chunks - 1)
            for h in range(GROUPS):
                if h + 1 < GROUPS:
                    probs(c, h + 1, (h + 1) % 2)
                else:
                    probs(c_next, 0, 0)
                acc_ref[h] += jnp.dot(vc, p_ref[h % 2], preferred_element_type=F32)
            return carry

        lax.fori_loop(0, n_chunks, chunk, 0, unroll=CHUNK_UNROLL)

    @pl.when(jnp.logical_not(bound_max <= MAX_SAFE_BOUND))
    def _():
        m_ref[...] = jnp.full(m_ref.shape, -jnp.inf, F32)
        for h in range(SCORES_AHEAD):
            s_ref[h] = scores(0, h)

        def chunk(c, carry):
            vc = vt_ref[c]
            c_next = jnp.minimum(c + 1, n_chunks - 1)
            for h in range(GROUPS):
                ahead = h + SCORES_AHEAD
                if ahead < GROUPS:
                    s_ref[ahead] = scores(c, ahead)
                else:
                    s_ref[ahead - GROUPS] = scores(c_next, ahead - GROUPS)
                s = s_ref[h]
                m_old = m_ref[h]
                slab_max = jnp.max(s.reshape(tk // MAX_SLAB, MAX_SLAB, tq), axis=0)
                m_new = jnp.maximum(m_old, jnp.max(slab_max, axis=0, keepdims=True))
                p = jnp.exp2(s - m_new)
                alpha = jnp.exp2(m_old - m_new)
                pv = jnp.dot(vc, p.astype(BF16), preferred_element_type=F32)
                acc_ref[h] = alpha * acc_ref[h] + pv
                m_ref[h] = m_new
            return carry

        lax.fori_loop(0, n_chunks, chunk, 0, unroll=2)

    outs = []
    for h in range(GROUPS):
        o = acc_ref[h, 0:HEAD_DIM, :] * (1.0 / acc_ref[h, HEAD_DIM:HEAD_DIM + 1, :])
        ms = jnp.mean(o * o, axis=0, keepdims=True)
        outs.append(o * lax.rsqrt(ms + EPS) * g_ref[h * HEAD_DIM:(h + 1) * HEAD_DIM, :])
    o_ref[...] = jnp.concatenate(outs, axis=0).T.astype(BF16)


def _attention(qt, k4, vt4, kn4, gain_t):
    b, _, s = qt.shape
    n_chunks, tk = k4.shape[1], k4.shape[2]
    tq = Q_BLOCK
    gw = GROUPS * HEAD_DIM
    return pl.pallas_call(
        _attn_kernel,
        out_shape=jax.ShapeDtypeStruct((b, s, ATTN_WIDTH), BF16),
        grid=(b, N_KV_HEADS, s // tq),
        in_specs=[pl.BlockSpec((None, gw, tq), lambda bi, g, qi: (bi, g, qi)),
                  pl.BlockSpec((None, n_chunks, tk, KV_WIDTH), lambda bi, g, qi: (bi, 0, 0, 0)),
                  pl.BlockSpec((None, n_chunks, None, V_ROWS, tk), lambda bi, g, qi: (bi, 0, g, 0, 0)),
                  pl.BlockSpec((None, None, 1, s), lambda bi, g, qi: (bi, g, 0, 0)),
                  pl.BlockSpec((None, gw, tq), lambda bi, g, qi: (g, 0, 0))],
        out_specs=pl.BlockSpec((None, tq, gw), lambda bi, g, qi: (bi, qi, g)),
        scratch_shapes=[pltpu.VMEM((GROUPS, 2 * HEAD_DIM, tq), BF16),
                        pltpu.VMEM((GROUPS, tk, tq), F32),
                        pltpu.VMEM((2, tk, tq), BF16),
                        pltpu.VMEM((GROUPS, 1, tq), F32),
                        pltpu.VMEM((GROUPS, V_ROWS, tq), F32)],
        compiler_params=_params("parallel", "parallel", "parallel"),
        name="attn",
    )(qt, k4, vt4, kn4, gain_t)


def _group_mean(sq, gmat):
    hi = sq.astype(BF16)
    lo = (sq - hi.astype(F32)).astype(BF16)
    return (jnp.dot(hi, gmat, preferred_element_type=F32) +
            jnp.dot(lo, gmat, preferred_element_type=F32))


def _convmod_kernel(prev_ref, cur_ref, next_ref, dw_ref, vec_ref, pw_ref, gmat_ref, o_ref, ext_ref, z_ref):
    i = pl.program_id(1)
    last = pl.num_programs(1) - 1
    t = cur_ref.shape[0]
    n_slab = cur_ref.shape[1] // LANES
    prev = jnp.where(i > 0, prev_ref[...], 0.0)
    nxt = jnp.where(i < last, next_ref[...], 0.0)
    for sl in range(n_slab):
        lanes = slice(sl * LANES, (sl + 1) * LANES)
        ext_ref[sl, 0:HALO, :] = prev[:, lanes]
        ext_ref[sl, HALO:HALO + t, :] = cur_ref[:, lanes]
        ext_ref[sl, HALO + t:, :] = nxt[:, lanes]

    dw_b, ln_g, ln_b, out_g = vec_ref[0:1, :], vec_ref[1:2, :], vec_ref[2:3, :], vec_ref[3:4, :]
    base = HALO - CONV_PAD
    rows_par = CONV_ROWS // 2
    for r in range(0, t, CONV_ROWS):
        for parity in range(2):
            slabs = []
            for sl in range(n_slab):
                lanes = slice(sl * LANES, (sl + 1) * LANES)
                acc = jnp.broadcast_to(dw_b[:, lanes], (rows_par, LANES))
                for j in range(CONV_WIDTH):
                    window = ext_ref[sl, pl.ds(r + parity + j + base, rows_par, stride=2), :]
                    acc = acc + dw_ref[j:j + 1, lanes] * window
                slabs.append(acc)
            y = _ln_noaffine(jnp.concatenate(slabs, axis=1)) * ln_g + ln_b
            z = y * jax.nn.sigmoid(y)
            for sl in range(n_slab):
                z_ref[sl, pl.ds(r + parity, rows_par, stride=2), :] = z[:, sl * LANES:(sl + 1) * LANES]

    z_rows = jnp.concatenate([z_ref[sl] for sl in range(n_slab)], axis=1).astype(BF16)
    pw = jnp.dot(z_rows, pw_ref[...], preferred_element_type=F32)
    ms = _group_mean(pw * pw, gmat_ref[...])
    o_ref[...] = (pw * lax.rsqrt(ms + EPS) * out_g).astype(BF16)


def _convmod(h, dw_w, vecs, pw_bf, gmat):
    b, s, ch = h.shape
    t = ROW_BLOCK
    hb = t // HALO
    n_halo = s // HALO
    return pl.pallas_call(
        _convmod_kernel,
        out_shape=jax.ShapeDtypeStruct((b, s, ch), BF16),
        grid=(b, s // t),
        in_specs=[pl.BlockSpec((None, HALO, ch), lambda bi, i: (bi, jnp.maximum(i * hb - 1, 0), 0)),
                  pl.BlockSpec((None, t, ch), lambda bi, i: (bi, i, 0)),
                  pl.BlockSpec((None, HALO, ch), lambda bi, i: (bi, jnp.minimum((i + 1) * hb, n_halo - 1), 0)),
                  pl.BlockSpec(dw_w.shape, lambda bi, i: (0, 0)),
                  pl.BlockSpec(vecs.shape, lambda bi, i: (0, 0)),
                  pl.BlockSpec(pw_bf.shape, lambda bi, i: (0, 0)),
                  pl.BlockSpec(gmat.shape, lambda bi, i: (0, 0))],
        out_specs=pl.BlockSpec((None, t, ch), lambda bi, i: (bi, i, 0)),
        scratch_shapes=[pltpu.VMEM((ch // LANES, t + 2 * HALO, LANES), F32),
                        pltpu.VMEM((ch // LANES, t, LANES), F32)],
        compiler_params=_params("parallel", "parallel"),
        name="convmod",
    )(h, h, h, dw_w, vecs, pw_bf, gmat)


def _outproj_kernel(alpha, x_ref, a_ref, c_ref, mod_ref, wa_ref, wc_ref, ln_ref, x1_ref, u2_ref, mix_ref):
    t = x_ref.shape[0]
    half = t // 2
    for r in (0, half):
        mix_ref[r:r + half, :] = (
            jnp.dot(a_ref[r:r + half, :], wa_ref[...], preferred_element_type=F32) +
            jnp.dot(c_ref[r:r + half, :], wc_ref[...], preferred_element_type=F32))
    gate1, shift2, scale2 = mod_ref[2:3, :], mod_ref[3:4, :], mod_ref[4:5, :]
    for r in range(0, t, NORM_ROWS):
        y = alpha * x_ref[r:r + NORM_ROWS, :] + gate1 * mix_ref[r:r + NORM_ROWS, :]
        x1 = _ln_noaffine(y) * ln_ref[0:1, :] + ln_ref[1:2, :]
        x1_ref[r:r + NORM_ROWS, :] = x1
        u2_ref[r:r + NORM_ROWS, :] = (_ln_noaffine(x1) * (1.0 + scale2) + shift2).astype(BF16)


def _outproj(alpha, x, attn, hc, mod, wo_a, wo_c, ln1):
    b, s, d = x.shape
    t = ROW_BLOCK
    aw, cw = attn.shape[2], hc.shape[2]
    return pl.pallas_call(
        functools.partial(_outproj_kernel, alpha),
        out_shape=(jax.ShapeDtypeStruct((b, s, d), F32),
                   jax.ShapeDtypeStruct((b, s, d), BF16)),
        grid=(b, s // t),
        in_specs=[pl.BlockSpec((None, t, d), lambda bi, i: (bi, i, 0)),
                  pl.BlockSpec((None, t, aw), lambda bi, i: (bi, i, 0)),
                  pl.BlockSpec((None, t, cw), lambda bi, i: (bi, i, 0)),
                  pl.BlockSpec((None, 6, d), lambda bi, i: (bi, 0, 0)),
                  pl.BlockSpec((aw, d), lambda bi, i: (0, 0)),
                  pl.BlockSpec((cw, d), lambda bi, i: (0, 0)),
                  pl.BlockSpec((2, d), lambda bi, i: (0, 0))],
        out_specs=(pl.BlockSpec((None, t, d), lambda bi, i: (bi, i, 0)),
                   pl.BlockSpec((None, t, d), lambda bi, i: (bi, i, 0))),
        scratch_shapes=[pltpu.VMEM((t, d), F32)],
        compiler_params=_params("parallel", "parallel"),
        name="outproj",
    )(x, attn, hc, mod, wo_a, wo_c, ln1)


def _ffn_kernel(alpha, prev_ref, cur_ref, next_ref, x1_ref, mod_ref, wup_ref, dw_ref, wdn_ref, ln_ref,
                o_ref, ext_ref, hv_ref, hg_ref, act_ref, acc_ref):
    i = pl.program_id(1)
    last = pl.num_programs(1) - 1
    t = cur_ref.shape[0]
    d_ff = wdn_ref.shape[0]
    ext_ref[0:HALO, :] = jnp.where(i > 0, prev_ref[...], jnp.zeros_like(prev_ref[...]))
    ext_ref[HALO:HALO + t, :] = cur_ref[...]
    ext_ref[HALO + t:, :] = jnp.where(i < last, next_ref[...], jnp.zeros_like(next_ref[...]))
    ext = ext_ref[...]

    piece = t // FFN_PIECES
    rows_par = piece // 2
    n_slab = FF_CHUNK // LANES

    def conv3(h_ref, slot, sl, col, first):
        w = dw_ref[:, col + sl * LANES:col + (sl + 1) * LANES]
        taps = [h_ref[slot, sl, pl.ds(first + k, rows_par, stride=2), :] for k in range(FFN_CONV_WIDTH)]
        return w[0:1] * taps[0] + w[1:2] * taps[1] + w[2:3] * taps[2] + w[3:4]

    def up_into(h_ref, c, slot):
        r = jnp.dot(ext, wup_ref[:, c:c + FF_CHUNK], preferred_element_type=F32)
        for sl in range(n_slab):
            h_ref[slot, sl] = r[:, sl * LANES:(sl + 1) * LANES]

    def up_val(c, slot):
        up_into(hv_ref, c, slot)

    def up_gate(c, slot):
        up_into(hg_ref, d_ff + c, slot)

    issue_before = {0: up_val, FFN_PIECES // 2: up_gate}
    up_val(0, 0)
    up_gate(0, 0)
    acc_ref[...] = jnp.zeros(acc_ref.shape, F32)
    for n, c in enumerate(range(0, d_ff, FF_CHUNK)):
        slot = n % 2
        for pi in range(FFN_PIECES):
            if c + FF_CHUNK < d_ff and pi in issue_before:
                issue_before[pi](c + FF_CHUNK, 1 - slot)
            r0 = pi * piece
            for sl in range(n_slab):
                for parity in range(2):
                    first = HALO + r0 + parity - 1
                    val = conv3(hv_ref, slot, sl, c, first)
                    gt = conv3(hg_ref, slot, sl, d_ff + c, first)
                    act = 0.5 * gt * (1.0 + lax.erf(gt * (2.0 ** -0.5))) * val
                    act_ref[sl, pl.ds(r0 + parity, rows_par, stride=2), :] = act
            act_rows = jnp.concatenate([act_ref[sl, r0:r0 + piece, :] for sl in range(n_slab)], axis=1)
            acc_ref[r0:r0 + piece, :] += jnp.dot(act_rows.astype(BF16), wdn_ref[c:c + FF_CHUNK, :],
                                                 preferred_element_type=F32)

    gate2 = mod_ref[5:6, :]
    for r in range(0, t, NORM_ROWS):
        y = alpha * x1_ref[r:r + NORM_ROWS, :] + gate2 * acc_ref[r:r + NORM_ROWS, :]
        o_ref[r:r + NORM_ROWS, :] = _ln_noaffine(y) * ln_ref[0:1, :] + ln_ref[1:2, :]


def _ffn(alpha, u2, x1, mod, wup_bf, dw4, wdn_bf, ln2):
    b, s, d = x1.shape
    t = ROW_BLOCK
    hb = t // HALO
    n_halo = s // HALO
    resident = dict(pipeline_mode=pl.Buffered(1))
    return pl.pallas_call(
        functools.partial(_ffn_kernel, alpha),
        out_shape=jax.ShapeDtypeStruct((b, s, d), F32),
        grid=(b, s // t),
        in_specs=[pl.BlockSpec((None, HALO, d), lambda bi, i: (bi, jnp.maximum(i * hb - 1, 0), 0)),
                  pl.BlockSpec((None, t, d), lambda bi, i: (bi, i, 0)),
                  pl.BlockSpec((None, HALO, d), lambda bi, i: (bi, jnp.minimum((i + 1) * hb, n_halo - 1), 0)),
                  pl.BlockSpec((None, t, d), lambda bi, i: (bi, i, 0)),
                  pl.BlockSpec((None, 6, d), lambda bi, i: (bi, 0, 0)),
                  pl.BlockSpec(wup_bf.shape, lambda bi, i: (0, 0), **resident),
                  pl.BlockSpec(dw4.shape, lambda bi, i: (0, 0)),
                  pl.BlockSpec(wdn_bf.shape, lambda bi, i: (0, 0), **resident),
                  pl.BlockSpec((2, d), lambda bi, i: (0, 0))],
        out_specs=pl.BlockSpec((None, t, d), lambda bi, i: (bi, i, 0)),
        scratch_shapes=[pltpu.VMEM((t + 2 * HALO, d), BF16),
                        pltpu.VMEM((2, FF_CHUNK // LANES, t + 2 * HALO, LANES), F32),
                        pltpu.VMEM((2, FF_CHUNK // LANES, t + 2 * HALO, LANES), F32),
                        pltpu.VMEM((FF_CHUNK // LANES, t, LANES), F32),
                        pltpu.VMEM((t, d), F32)],
        compiler_params=_params("parallel", "parallel"),
        name="ffn",
    )(u2, u2, u2, x1, mod, wup_bf, dw4, wdn_bf, ln2)


def _rope_table_t(seq_len):
    pos = jnp.arange(seq_len, dtype=jnp.int32)
    rows = (pos // GRID_W).astype(F32)
    cols = (pos % GRID_W).astype(F32)
    inv_freq = ROPE_THETA ** (-jnp.arange(ROPE_FREQS, dtype=F32) / ROPE_FREQS)
    ang_r = inv_freq[:, None] * rows[None, :]
    ang_c = inv_freq[:, None] * cols[None, :]
    return jnp.concatenate([jnp.cos(ang_r), jnp.sin(ang_r), jnp.cos(ang_c), jnp.sin(ang_c)], axis=0)


def kernel(x, c, w_ada, b_ada, w_in, q_norm_g, k_norm_g, conv_dw_w, conv_dw_b, conv_ln_g, conv_ln_b,
           w_conv_pw2, attn_out_g, conv_out_g, w_o, ln1_g, ln1_b, w_up, ffn_dw_w, ffn_dw_b, w_down,
           ln2_g, ln2_b):
    b, s, d = x.shape
    depth = w_ada.shape[0]
    alpha = (2.0 * depth) ** 0.25
    conv_ch = w_conv_pw2.shape[1]
    assert s % ROW_BLOCK == 0 and s % Q_BLOCK == 0 and s % GRID_W == 0
    assert w_down.shape[1] % FF_CHUNK == 0 and b <= 8

    tab = _rope_table_t(s)
    c_pad = jnp.zeros((8, d), F32).at[:b].set(c)
    gidx = jnp.arange(conv_ch) // CONV_GROUP_DIM
    gmat = jnp.where(gidx[:, None] == gidx[None, :], 1.0 / CONV_GROUP_DIM, 0.0).astype(BF16)

    for l in range(depth):
        mod = _adaln(c_pad, w_ada[l], b_ada[l][None, :])[:b].reshape(b, 6, d)

        gains = jnp.stack([jnp.broadcast_to(q_norm_g[l][:, None], (HEAD_DIM, ROW_BLOCK)),
                           jnp.broadcast_to(k_norm_g[l][:, None], (HEAD_DIM, ROW_BLOCK))])
        qt, k, vt4, h, kn = _inproj(x, mod, w_in[l].astype(BF16), gains, tab)

        k4 = k.reshape(b, s // KV_CHUNK, KV_CHUNK, KV_WIDTH)
        gain_t = jnp.broadcast_to(attn_out_g[l].reshape(N_KV_HEADS, GROUPS * HEAD_DIM, 1),
                                  (N_KV_HEADS, GROUPS * HEAD_DIM, Q_BLOCK))
        attn = _attention(qt, k4, vt4, kn.reshape(b, N_KV_HEADS, 1, s), gain_t)

        dw_w = jnp.zeros((CONV_WIDTH + 1, conv_ch), F32).at[:CONV_WIDTH].set(conv_dw_w[l])
        vecs = jnp.stack([conv_dw_b[l], conv_ln_g[l], conv_ln_b[l], conv_out_g[l].reshape(-1)])
        hc = _convmod(h, dw_w, vecs, w_conv_pw2[l].astype(BF16), gmat)

        wo_bf = w_o[l].astype(BF16)
        x1, u2 = _outproj(alpha, x, attn, hc, mod, wo_bf[:ATTN_WIDTH], wo_bf[ATTN_WIDTH:],
                          jnp.stack([ln1_g[l], ln1_b[l]]))

        dw4 = jnp.concatenate([ffn_dw_w[l], ffn_dw_b[l][None, :]], axis=0)
        x = _ffn(alpha, u2, x1, mod, w_up[l].astype(BF16), dw4, w_down[l].astype(BF16),
                 jnp.stack([ln2_g[l], ln2_b[l]]))
    return x
```

```python
import functools

import jax
import jax.numpy as jnp
from jax import lax
from jax.experimental import pallas as pl
from jax.experimental.pallas import tpu as pltpu

GRID_W = 64
HEAD_DIM = 64
N_HEADS = 8
N_KV_HEADS = 2
GROUPS = N_HEADS // N_KV_HEADS
ATTN_WIDTH = N_HEADS * HEAD_DIM
KV_WIDTH = N_KV_HEADS * HEAD_DIM
CONV_GROUP_DIM = 64
CONV_WIDTH = 31
CONV_PAD = (CONV_WIDTH - 1) // 2
FFN_CONV_WIDTH = 3
ROPE_THETA = 10000.0
ROPE_FREQS = HEAD_DIM // 4
EPS = 1e-6
Q_SCALE = HEAD_DIM ** -0.5 * 1.4426950408889634
SUM_ROWS = 16
V_ROWS = HEAD_DIM + SUM_ROWS

F32 = jnp.float32
BF16 = jnp.bfloat16

V7X_VMEM_LIMIT_BYTES = 56 * 1024 * 1024
ROW_BLOCK = 512
Q_BLOCK = 512
KV_CHUNK = ROW_BLOCK
LANES = 128
HALO = 16
CONV_ROWS = 32
FF_CHUNK = 256
FFN_PIECES = 2
NORM_ROWS = 32
MAX_SLAB = 64
SCORES_AHEAD = 2
CHUNK_UNROLL = 8
MAX_SAFE_BOUND = 60.0
BOUND_SLACK = 1.0 + 2.0 ** -6


def _params(*sem, fuse_inputs=None):
    return pltpu.CompilerParams(dimension_semantics=sem, vmem_limit_bytes=V7X_VMEM_LIMIT_BYTES,
                                allow_input_fusion=fuse_inputs)


def _ln_noaffine(x):
    mu = jnp.mean(x, axis=-1, keepdims=True)
    xc = x - mu
    var = jnp.mean(xc * xc, axis=-1, keepdims=True)
    return xc * lax.rsqrt(var + EPS)


def _adaln_kernel(c_ref, w_ref, b_ref, o_ref):
    c = c_ref[...]
    c_act = c * jax.nn.sigmoid(c)
    w = w_ref[...]
    c_hi, w_hi = c_act.astype(BF16), w.astype(BF16)
    c_lo = (c_act - c_hi.astype(F32)).astype(BF16)
    w_lo = (w - w_hi.astype(F32)).astype(BF16)
    o_ref[...] = (jnp.dot(c_hi, w_hi, preferred_element_type=F32) +
                  jnp.dot(c_hi, w_lo, preferred_element_type=F32) +
                  jnp.dot(c_lo, w_hi, preferred_element_type=F32)) + b_ref[...]


def _adaln(c_pad, w, b):
    rows, d = c_pad.shape
    n = w.shape[1]
    cols = 2 * d
    return pl.pallas_call(
        _adaln_kernel,
        out_shape=jax.ShapeDtypeStruct((rows, n), F32),
        grid=(n // cols,),
        in_specs=[pl.BlockSpec((rows, d), lambda j: (0, 0)),
                  pl.BlockSpec((d, cols), lambda j: (0, j)),
                  pl.BlockSpec((1, cols), lambda j: (0, j))],
        out_specs=pl.BlockSpec((rows, cols), lambda j: (0, j)),
        compiler_params=_params("arbitrary"),
        name="adaln",
    )(c_pad, w, b)


def _rope_t(x, tab):
    f = ROPE_FREQS
    cr, sr, cc, sc = tab[0:f], tab[f:2 * f], tab[2 * f:3 * f], tab[3 * f:4 * f]
    x1r, x2r, x1c, x2c = x[0:f], x[f:2 * f], x[2 * f:3 * f], x[3 * f:4 * f]
    return jnp.concatenate([x1r * cr - x2r * sr, x2r * cr + x1r * sr,
                            x1c * cc - x2c * sc, x2c * cc + x1c * sc], axis=0)


def _norm_rope_t(xt, gain, tab):
    ms = jnp.mean(xt * xt, axis=0, keepdims=True)
    return _rope_t(xt * lax.rsqrt(ms + EPS) * gain, tab)


def _inproj_kernel(x_ref, mod_ref, w_ref, gains_ref, tab_ref, qt_ref, k_ref, vt_ref, h_ref, kn_ref,
                   u_ref, proj_ref):
    shift, scale = mod_ref[0:1, :], mod_ref[1:2, :]
    t = x_ref.shape[0]
    half = t // 2
    qkv_w = ATTN_WIDTH + 2 * KV_WIDTH
    conv_ch = (w_ref.shape[1] - qkv_w) // 2

    for r0 in (0, half):
        for r in range(r0, r0 + half, NORM_ROWS):
            u = _ln_noaffine(x_ref[r:r + NORM_ROWS, :]) * (1.0 + scale) + shift
            u_ref[r:r + NORM_ROWS, :] = u.astype(BF16)
        proj_ref[r0:r0 + half, :] = jnp.dot(u_ref[r0:r0 + half, :], w_ref[...],
                                            preferred_element_type=F32)

    for r0 in (0, half):
        cols = slice(r0, r0 + half)
        qkv_t = proj_ref[r0:r0 + half, 0:qkv_w].T
        tab = tab_ref[:, cols]
        gq, gk = gains_ref[0, :, 0:half], gains_ref[1, :, 0:half]
        for h in range(N_HEADS):
            r = h * HEAD_DIM
            qh = _norm_rope_t(qkv_t[r:r + HEAD_DIM], gq, tab) * Q_SCALE
            qt_ref[r:r + HEAD_DIM, cols] = qh.astype(BF16)
        k_rot = []
        for h in range(N_KV_HEADS):
            r = ATTN_WIDTH + h * HEAD_DIM
            kh = _norm_rope_t(qkv_t[r:r + HEAD_DIM], gk, tab).astype(BF16)
            k_rot.append(kh)
            khf = kh.astype(F32)
            kn_ref[h:h + 1, cols] = jnp.sum(khf * khf, axis=0, keepdims=True)
        k_ref[cols, :] = jnp.concatenate(k_rot, axis=0).astype(F32).T.astype(BF16)
        for h in range(N_KV_HEADS):
            r = ATTN_WIDTH + KV_WIDTH + h * HEAD_DIM
            vt_ref[h, 0:HEAD_DIM, cols] = qkv_t[r:r + HEAD_DIM].astype(BF16)
            vt_ref[h, HEAD_DIM:, cols] = jnp.ones((SUM_ROWS, half), BF16)

        a = proj_ref[r0:r0 + half, qkv_w:qkv_w + conv_ch]
        g = proj_ref[r0:r0 + half, qkv_w + conv_ch:]
        h_ref[r0:r0 + half, :] = a * jax.nn.sigmoid(g)


def _inproj(x, mod, w_bf, gains, tab):
    b, s, d = x.shape
    t = ROW_BLOCK
    n_cols = w_bf.shape[1]
    conv_ch = (n_cols - ATTN_WIDTH - 2 * KV_WIDTH) // 2
    return pl.pallas_call(
        _inproj_kernel,
        out_shape=(jax.ShapeDtypeStruct((b, ATTN_WIDTH, s), BF16),
                   jax.ShapeDtypeStruct((b, s, KV_WIDTH), BF16),
                   jax.ShapeDtypeStruct((b, s // t, N_KV_HEADS, V_ROWS, t), BF16),
                   jax.ShapeDtypeStruct((b, s, conv_ch), F32),
                   jax.ShapeDtypeStruct((b, N_KV_HEADS, s), F32)),
        grid=(b, s // t),
        in_specs=[pl.BlockSpec((None, t, d), lambda bi, i: (bi, i, 0)),
                  pl.BlockSpec((None, 6, d), lambda bi, i: (bi, 0, 0)),
                  pl.BlockSpec((d, n_cols), lambda bi, i: (0, 0)),
                  pl.BlockSpec((2, HEAD_DIM, t), lambda bi, i: (0, 0, 0)),
                  pl.BlockSpec((HEAD_DIM, t), lambda bi, i: (0, i))],
        out_specs=(pl.BlockSpec((None, ATTN_WIDTH, t), lambda bi, i: (bi, 0, i)),
                   pl.BlockSpec((None, t, KV_WIDTH), lambda bi, i: (bi, i, 0)),
                   pl.BlockSpec((None, None, N_KV_HEADS, V_ROWS, t), lambda bi, i: (bi, i, 0, 0, 0)),
                   pl.BlockSpec((None, t, conv_ch), lambda bi, i: (bi, i, 0)),
                   pl.BlockSpec((None, N_KV_HEADS, t), lambda bi, i: (bi, 0, i))),
        scratch_shapes=[pltpu.VMEM((t, d), BF16),
                        pltpu.VMEM((t, n_cols), F32)],
        compiler_params=_params("parallel", "parallel", fuse_inputs=[False, False, True, False, False]),
        name="inproj",
    )(x, mod, w_bf, gains, tab)


def _attn_kernel(qt_ref, k_ref, vt_ref, kn_ref, g_ref, o_ref, qp_ref, s_ref, p_ref, m_ref, acc_ref):
    grp = pl.program_id(1)
    n_chunks, tk = k_ref.shape[0], k_ref.shape[1]
    tq = qt_ref.shape[1]

    first = grp == 0
    k_norm2_max = jnp.max(kn_ref[...], axis=1, keepdims=True)
    bounds = []
    for h in range(GROUPS):
        qh = qt_ref[h * HEAD_DIM:(h + 1) * HEAD_DIM, :]
        z = jnp.zeros_like(qh)
        qp_ref[h, 0:HEAD_DIM, :] = jnp.where(first, qh, z)
        qp_ref[h, HEAD_DIM:2 * HEAD_DIM, :] = jnp.where(first, z, qh)
        qf = qh.astype(F32)
        q_norm2 = jnp.sum(qf * qf, axis=0, keepdims=True)
        bounds.append(jnp.sqrt(q_norm2 * k_norm2_max) * BOUND_SLACK)
    acc_ref[...] = jnp.zeros(acc_ref.shape, F32)
    bound_max = jnp.max(jnp.concatenate(bounds, axis=0))

    def scores(c, h):
        return jnp.dot(k_ref[c], qp_ref[h], preferred_element_type=F32)

    @pl.when(bound_max <= MAX_SAFE_BOUND)
    def _():
        for h in range(GROUPS):
            m_ref[h] = bounds[h]

        def probs(c, h, slot):
            p_ref[slot] = jnp.exp2(scores(c, h) - m_ref[h]).astype(BF16)

        probs(0, 0, 0)

        def chunk(c, carry):
            vc = vt_ref[c]
            c_next = jnp.minimum(c + 1, n_chunks - 1)
            for h in range(GROUPS):
                if h + 1 < GROUPS:
                    probs(c, h + 1, (h + 1) % 2)
                else:
                    probs(c_next, 0, 0)
                acc_ref[h] += jnp.dot(vc, p_ref[h % 2], preferred_element_type=F32)
            return carry

        lax.fori_loop(0, n_chunks, chunk, 0, unroll=CHUNK_UNROLL)

    @pl.when(jnp.logical_not(bound_max <= MAX_SAFE_BOUND))
    def _():
        m_ref[...] = jnp.full(m_ref.shape, -jnp.inf, F32)
        for h in range(SCORES_AHEAD):
            s_ref[h] = scores(0, h)

        def chunk(c, carry):
            vc = vt_ref[c]
            c_next = jnp.minimum(c + 1, n_chunks - 1)
            for h in range(GROUPS):
                ahead = h + SCORES_AHEAD
                if ahead < GROUPS:
                    s_ref[ahead] = scores(c, ahead)
                else:
                    s_ref[ahead - GROUPS] = scores(c_next, ahead - GROUPS)
                s = s_ref[h]
                m_old = m_ref[h]
                slab_max = jnp.max(s.reshape(tk // MAX_SLAB, MAX_SLAB, tq), axis=0)
                m_new = jnp.maximum(m_old, jnp.max(slab_max, axis=0, keepdims=True))
                p = jnp.exp2(s - m_new)
                alpha = jnp.exp2(m_old - m_new)
                pv = jnp.dot(vc, p.astype(BF16), preferred_element_type=F32)
                acc_ref[h] = alpha * acc_ref[h] + pv
                m_ref[h] = m_new
            return carry

        lax.fori_loop(0, n_chunks, chunk, 0, unroll=2)

    outs = []
    for h in range(GROUPS):
        o = acc_ref[h, 0:HEAD_DIM, :] * (1.0 / acc_ref[h, HEAD_DIM:HEAD_DIM + 1, :])
        ms = jnp.mean(o * o, axis=0, keepdims=True)
        outs.append(o * lax.rsqrt(ms + EPS) * g_ref[h * HEAD_DIM:(h + 1) * HEAD_DIM, :])
    o_ref[...] = jnp.concatenate(outs, axis=0).T.astype(BF16)


def _attention(qt, k4, vt4, kn4, gain_t):
    b, _, s = qt.shape
    n_chunks, tk = k4.shape[1], k4.shape[2]
    tq = Q_BLOCK
    gw = GROUPS * HEAD_DIM
    return pl.pallas_call(
        _attn_kernel,
        out_shape=jax.ShapeDtypeStruct((b, s, ATTN_WIDTH), BF16),
        grid=(b, N_KV_HEADS, s // tq),
        in_specs=[pl.BlockSpec((None, gw, tq), lambda bi, g, qi: (bi, g, qi)),
                  pl.BlockSpec((None, n_chunks, tk, KV_WIDTH), lambda bi, g, qi: (bi, 0, 0, 0)),
                  pl.BlockSpec((None, n_chunks, None, V_ROWS, tk), lambda bi, g, qi: (bi, 0, g, 0, 0)),
                  pl.BlockSpec((None, None, 1, s), lambda bi, g, qi: (bi, g, 0, 0)),
                  pl.BlockSpec((None, gw, tq), lambda bi, g, qi: (g, 0, 0))],
        out_specs=pl.BlockSpec((None, tq, gw), lambda bi, g, qi: (bi, qi, g)),
        scratch_shapes=[pltpu.VMEM((GROUPS, 2 * HEAD_DIM, tq), BF16),
                        pltpu.VMEM((GROUPS, tk, tq), F32),
                        pltpu.VMEM((2, tk, tq), BF16),
                        pltpu.VMEM((GROUPS, 1, tq), F32),
                        pltpu.VMEM((GROUPS, V_ROWS, tq), F32)],
        compiler_params=_params("parallel", "parallel", "parallel"),
        name="attn",
    )(qt, k4, vt4, kn4, gain_t)


def _group_mean(sq, gmat):
    hi = sq.astype(BF16)
    lo = (sq - hi.astype(F32)).astype(BF16)
    return (jnp.dot(hi, gmat, preferred_element_type=F32) +
            jnp.dot(lo, gmat, preferred_element_type=F32))


def _convmod_kernel(prev_ref, cur_ref, next_ref, dw_ref, vec_ref, pw_ref, gmat_ref, o_ref, ext_ref, z_ref):
    i = pl.program_id(1)
    last = pl.num_programs(1) - 1
    t = cur_ref.shape[0]
    n_slab = cur_ref.shape[1] // LANES
    prev = jnp.where(i > 0, prev_ref[...], 0.0)
    nxt = jnp.where(i < last, next_ref[...], 0.0)
    for sl in range(n_slab):
        lanes = slice(sl * LANES, (sl + 1) * LANES)
        ext_ref[sl, 0:HALO, :] = prev[:, lanes]
        ext_ref[sl, HALO:HALO + t, :] = cur_ref[:, lanes]
        ext_ref[sl, HALO + t:, :] = nxt[:, lanes]

    dw_b, ln_g, ln_b, out_g = vec_ref[0:1, :], vec_ref[1:2, :], vec_ref[2:3, :], vec_ref[3:4, :]
    base = HALO - CONV_PAD
    rows_par = CONV_ROWS // 2
    for r in range(0, t, CONV_ROWS):
        for parity in range(2):
            slabs = []
            for sl in range(n_slab):
                lanes = slice(sl * LANES, (sl + 1) * LANES)
                acc = jnp.broadcast_to(dw_b[:, lanes], (rows_par, LANES))
                for j in range(CONV_WIDTH):
                    window = ext_ref[sl, pl.ds(r + parity + j + base, rows_par, stride=2), :]
                    acc = acc + dw_ref[j:j + 1, lanes] * window
                slabs.append(acc)
            y = _ln_noaffine(jnp.concatenate(slabs, axis=1)) * ln_g + ln_b
            z = y * jax.nn.sigmoid(y)
            for sl in range(n_slab):
                z_ref[sl, pl.ds(r + parity, rows_par, stride=2), :] = z[:, sl * LANES:(sl + 1) * LANES]

    z_rows = jnp.concatenate([z_ref[sl] for sl in range(n_slab)], axis=1).astype(BF16)
    pw = jnp.dot(z_rows, pw_ref[...], preferred_element_type=F32)
    ms = _group_mean(pw * pw, gmat_ref[...])
    o_ref[...] = (pw * lax.rsqrt(ms + EPS) * out_g).astype(BF16)


def _convmod(h, dw_w, vecs, pw_bf, gmat):
    b, s, ch = h.shape
    t = ROW_BLOCK
    hb = t // HALO
    n_halo = s // HALO
    return pl.pallas_call(
        _convmod_kernel,
        out_shape=jax.ShapeDtypeStruct((b, s, ch), BF16),
        grid=(b, s // t),
        in_specs=[pl.BlockSpec((None, HALO, ch), lambda bi, i: (bi, jnp.maximum(i * hb - 1, 0), 0)),
                  pl.BlockSpec((None, t, ch), lambda bi, i: (bi, i, 0)),
                  pl.BlockSpec((None, HALO, ch), lambda bi, i: (bi, jnp.minimum((i + 1) * hb, n_halo - 1), 0)),
                  pl.BlockSpec(dw_w.shape, lambda bi, i: (0, 0)),
                  pl.BlockSpec(vecs.shape, lambda bi, i: (0, 0)),
                  pl.BlockSpec(pw_bf.shape, lambda bi, i: (0, 0)),
                  pl.BlockSpec(gmat.shape, lambda bi, i: (0, 0))],
        out_specs=pl.BlockSpec((None, t, ch), lambda bi, i: (bi, i, 0)),
        scratch_shapes=[pltpu.VMEM((ch // LANES, t + 2 * HALO, LANES), F32),
                        pltpu.VMEM((ch // LANES, t, LANES), F32)],
        compiler_params=_params("parallel", "parallel",
                                fuse_inputs=[False, False, False, False, False, True, False]),
        name="convmod",
    )(h, h, h, dw_w, vecs, pw_bf, gmat)


def _outproj_kernel(alpha, x_ref, a_ref, c_ref, mod_ref, wa_ref, wc_ref, ln_ref, x1_ref, u2_ref, mix_ref):
    t = x_ref.shape[0]
    half = t // 2
    for r in (0, half):
        mix_ref[r:r + half, :] = (
            jnp.dot(a_ref[r:r + half, :], wa_ref[...], preferred_element_type=F32) +
            jnp.dot(c_ref[r:r + half, :], wc_ref[...], preferred_element_type=F32))
    gate1, shift2, scale2 = mod_ref[2:3, :], mod_ref[3:4, :], mod_ref[4:5, :]
    for r in range(0, t, NORM_ROWS):
        y = alpha * x_ref[r:r + NORM_ROWS, :] + gate1 * mix_ref[r:r + NORM_ROWS, :]
        x1 = _ln_noaffine(y) * ln_ref[0:1, :] + ln_ref[1:2, :]
        x1_ref[r:r + NORM_ROWS, :] = x1
        u2_ref[r:r + NORM_ROWS, :] = (_ln_noaffine(x1) * (1.0 + scale2) + shift2).astype(BF16)


def _outproj(alpha, x, attn, hc, mod, wo_a, wo_c, ln1):
    b, s, d = x.shape
    t = ROW_BLOCK
    aw, cw = attn.shape[2], hc.shape[2]
    return pl.pallas_call(
        functools.partial(_outproj_kernel, alpha),
        out_shape=(jax.ShapeDtypeStruct((b, s, d), F32),
                   jax.ShapeDtypeStruct((b, s, d), BF16)),
        grid=(b, s // t),
        in_specs=[pl.BlockSpec((None, t, d), lambda bi, i: (bi, i, 0)),
                  pl.BlockSpec((None, t, aw), lambda bi, i: (bi, i, 0)),
                  pl.BlockSpec((None, t, cw), lambda bi, i: (bi, i, 0)),
                  pl.BlockSpec((None, 6, d), lambda bi, i: (bi, 0, 0)),
                  pl.BlockSpec((aw, d), lambda bi, i: (0, 0)),
                  pl.BlockSpec((cw, d), lambda bi, i: (0, 0)),
                  pl.BlockSpec((2, d), lambda bi, i: (0, 0))],
        out_specs=(pl.BlockSpec((None, t, d), lambda bi, i: (bi, i, 0)),
                   pl.BlockSpec((None, t, d), lambda bi, i: (bi, i, 0))),
        scratch_shapes=[pltpu.VMEM((t, d), F32)],
        compiler_params=_params("parallel", "parallel",
                                fuse_inputs=[False, False, False, False, True, True, False]),
        name="outproj",
    )(x, attn, hc, mod, wo_a, wo_c, ln1)


def _ffn_kernel(alpha, prev_ref, cur_ref, next_ref, x1_ref, mod_ref, wup_ref, dw_ref, wdn_ref, ln_ref,
                o_ref, ext_ref, hv_ref, hg_ref, act_ref, acc_ref):
    i = pl.program_id(1)
    last = pl.num_programs(1) - 1
    t = cur_ref.shape[0]
    d_ff = wdn_ref.shape[0]
    ext_ref[0:HALO, :] = jnp.where(i > 0, prev_ref[...], jnp.zeros_like(prev_ref[...]))
    ext_ref[HALO:HALO + t, :] = cur_ref[...]
    ext_ref[HALO + t:, :] = jnp.where(i < last, next_ref[...], jnp.zeros_like(next_ref[...]))
    ext = ext_ref[...]

    piece = t // FFN_PIECES
    rows_par = piece // 2
    n_slab = FF_CHUNK // LANES

    def conv3(h_ref, slot, sl, col, first):
        w = dw_ref[:, col + sl * LANES:col + (sl + 1) * LANES]
        taps = [h_ref[slot, sl, pl.ds(first + k, rows_par, stride=2), :] for k in range(FFN_CONV_WIDTH)]
        return w[0:1] * taps[0] + w[1:2] * taps[1] + w[2:3] * taps[2] + w[3:4]

    def up_into(h_ref, c, slot):
        r = jnp.dot(ext, wup_ref[:, c:c + FF_CHUNK], preferred_element_type=F32)
        for sl in range(n_slab):
            h_ref[slot, sl] = r[:, sl * LANES:(sl + 1) * LANES]

    def up_val(c, slot):
        up_into(hv_ref, c, slot)

    def up_gate(c, slot):
        up_into(hg_ref, d_ff + c, slot)

    issue_before = {0: up_val, FFN_PIECES // 2: up_gate}
    up_val(0, 0)
    up_gate(0, 0)
    acc_ref[...] = jnp.zeros(acc_ref.shape, F32)
    for n, c in enumerate(range(0, d_ff, FF_CHUNK)):
        slot = n % 2
        for pi in range(FFN_PIECES):
            if c + FF_CHUNK < d_ff and pi in issue_before:
                issue_before[pi](c + FF_CHUNK, 1 - slot)
            r0 = pi * piece
            for sl in range(n_slab):
                for parity in range(2):
                    first = HALO + r0 + parity - 1
                    val = conv3(hv_ref, slot, sl, c, first)
                    gt = conv3(hg_ref, slot, sl, d_ff + c, first)
                    act = 0.5 * gt * (1.0 + lax.erf(gt * (2.0 ** -0.5))) * val
                    act_ref[sl, pl.ds(r0 + parity, rows_par, stride=2), :] = act
            act_rows = jnp.concatenate([act_ref[sl, r0:r0 + piece, :] for sl in range(n_slab)], axis=1)
            acc_ref[r0:r0 + piece, :] += jnp.dot(act_rows.astype(BF16), wdn_ref[c:c + FF_CHUNK, :],
                                                 preferred_element_type=F32)

    gate2 = mod_ref[5:6, :]
    for r in range(0, t, NORM_ROWS):
        y = alpha * x1_ref[r:r + NORM_ROWS, :] + gate2 * acc_ref[r:r + NORM_ROWS, :]
        o_ref[r:r + NORM_ROWS, :] = _ln_noaffine(y) * ln_ref[0:1, :] + ln_ref[1:2, :]


def _ffn(alpha, u2, x1, mod, wup_bf, dw4, wdn_bf, ln2):
    b, s, d = x1.shape
    t = ROW_BLOCK
    hb = t // HALO
    n_halo = s // HALO
    resident = dict(pipeline_mode=pl.Buffered(1))
    return pl.pallas_call(
        functools.partial(_ffn_kernel, alpha),
        out_shape=jax.ShapeDtypeStruct((b, s, d), F32),
        grid=(b, s // t),
        in_specs=[pl.BlockSpec((None, HALO, d), lambda bi, i: (bi, jnp.maximum(i * hb - 1, 0), 0)),
                  pl.BlockSpec((None, t, d), lambda bi, i: (bi, i, 0)),
                  pl.BlockSpec((None, HALO, d), lambda bi, i: (bi, jnp.minimum((i + 1) * hb, n_halo - 1), 0)),
                  pl.BlockSpec((None, t, d), lambda bi, i: (bi, i, 0)),
                  pl.BlockSpec((None, 6, d), lambda bi, i: (bi, 0, 0)),
                  pl.BlockSpec(wup_bf.shape, lambda bi, i: (0, 0), **resident),
                  pl.BlockSpec(dw4.shape, lambda bi, i: (0, 0)),
                  pl.BlockSpec(wdn_bf.shape, lambda bi, i: (0, 0), **resident),
                  pl.BlockSpec((2, d), lambda bi, i: (0, 0))],
        out_specs=pl.BlockSpec((None, t, d), lambda bi, i: (bi, i, 0)),
        scratch_shapes=[pltpu.VMEM((t + 2 * HALO, d), BF16),
                        pltpu.VMEM((2, FF_CHUNK // LANES, t + 2 * HALO, LANES), F32),
                        pltpu.VMEM((2, FF_CHUNK // LANES, t + 2 * HALO, LANES), F32),
                        pltpu.VMEM((FF_CHUNK // LANES, t, LANES), F32),
                        pltpu.VMEM((t, d), F32)],
        compiler_params=_params("parallel", "parallel"),
        name="ffn",
    )(u2, u2, u2, x1, mod, wup_bf, dw4, wdn_bf, ln2)


def _rope_table_t(seq_len):
    pos = jnp.arange(seq_len, dtype=jnp.int32)
    rows = (pos // GRID_W).astype(F32)
    cols = (pos % GRID_W).astype(F32)
    inv_freq = ROPE_THETA ** (-jnp.arange(ROPE_FREQS, dtype=F32) / ROPE_FREQS)
    ang_r = inv_freq[:, None] * rows[None, :]
    ang_c = inv_freq[:, None] * cols[None, :]
    return jnp.concatenate([jnp.cos(ang_r), jnp.sin(ang_r), jnp.cos(ang_c), jnp.sin(ang_c)], axis=0)


def kernel(x, c, w_ada, b_ada, w_in, q_norm_g, k_norm_g, conv_dw_w, conv_dw_b, conv_ln_g, conv_ln_b,
           w_conv_pw2, attn_out_g, conv_out_g, w_o, ln1_g, ln1_b, w_up, ffn_dw_w, ffn_dw_b, w_down,
           ln2_g, ln2_b):
    b, s, d = x.shape
    depth = w_ada.shape[0]
    alpha = (2.0 * depth) ** 0.25
    conv_ch = w_conv_pw2.shape[1]
    assert s % ROW_BLOCK == 0 and s % Q_BLOCK == 0 and s % GRID_W == 0
    assert w_down.shape[1] % FF_CHUNK == 0 and b <= 8

    tab = _rope_table_t(s)
    c_pad = jnp.zeros((8, d), F32).at[:b].set(c)
    gidx = jnp.arange(conv_ch) // CONV_GROUP_DIM
    gmat = jnp.where(gidx[:, None] == gidx[None, :], 1.0 / CONV_GROUP_DIM, 0.0).astype(BF16)

    for l in range(depth):
        mod = _adaln(c_pad, w_ada[l], b_ada[l][None, :])[:b].reshape(b, 6, d)

        gains = jnp.stack([jnp.broadcast_to(q_norm_g[l][:, None], (HEAD_DIM, ROW_BLOCK)),
                           jnp.broadcast_to(k_norm_g[l][:, None], (HEAD_DIM, ROW_BLOCK))])
        qt, k, vt4, h, kn = _inproj(x, mod, w_in[l].astype(BF16), gains, tab)

        k4 = k.reshape(b, s // KV_CHUNK, KV_CHUNK, KV_WIDTH)
        gain_t = jnp.broadcast_to(attn_out_g[l].reshape(N_KV_HEADS, GROUPS * HEAD_DIM, 1),
                                  (N_KV_HEADS, GROUPS * HEAD_DIM, Q_BLOCK))
        attn = _attention(qt, k4, vt4, kn.reshape(b, N_KV_HEADS, 1, s), gain_t)

        dw_w = jnp.zeros((CONV_WIDTH + 1, conv_ch), F32).at[:CONV_WIDTH].set(conv_dw_w[l])
        vecs = jnp.stack([conv_dw_b[l], conv_ln_g[l], conv_ln_b[l], conv_out_g[l].reshape(-1)])
        hc = _convmod(h, dw_w, vecs, w_conv_pw2[l].astype(BF16), gmat)

        wo_bf = w_o[l].astype(BF16)
        x1, u2 = _outproj(alpha, x, attn, hc, mod, wo_bf[:ATTN_WIDTH], wo_bf[ATTN_WIDTH:],
                          jnp.stack([ln1_g[l], ln1_b[l]]))

        dw4 = jnp.concatenate([ffn_dw_w[l], ffn_dw_b[l][None, :]], axis=0)
        x = _ffn(alpha, u2, x1, mod, w_up[l].astype(BF16), dw4, w_down[l].astype(BF16),
                 jnp.stack([ln2_g[l], ln2_b[l]]))
    return x
```

```python
import functools

import jax
import jax.numpy as jnp
from jax import lax
from jax.experimental import pallas as pl
from jax.experimental.pallas import tpu as pltpu

GRID_W = 64
HEAD_DIM = 64
N_HEADS = 8
N_KV_HEADS = 2
GROUPS = N_HEADS // N_KV_HEADS
ATTN_WIDTH = N_HEADS * HEAD_DIM
KV_WIDTH = N_KV_HEADS * HEAD_DIM
CONV_GROUP_DIM = 64
CONV_WIDTH = 31
CONV_PAD = (CONV_WIDTH - 1) // 2
FFN_CONV_WIDTH = 3
ROPE_THETA = 10000.0
ROPE_FREQS = HEAD_DIM // 4
EPS = 1e-6
Q_SCALE = HEAD_DIM ** -0.5 * 1.4426950408889634
SUM_ROWS = 16
V_ROWS = HEAD_DIM + SUM_ROWS

F32 = jnp.float32
BF16 = jnp.bfloat16

V7X_VMEM_LIMIT_BYTES = 56 * 1024 * 1024
ROW_BLOCK = 512
Q_BLOCK = 512
KV_CHUNK = ROW_BLOCK
LANES = 128
HALO = 16
CONV_ROWS = 32
FF_CHUNK = 256
FFN_PIECES = 2
NORM_ROWS = 32
MAX_SLAB = 64
SCORES_AHEAD = 2
CHUNK_UNROLL = 8
MAX_SAFE_BOUND = 60.0
BOUND_SLACK = 1.0 + 2.0 ** -6


def _params(*sem, fuse_inputs=None):
    return pltpu.CompilerParams(dimension_semantics=sem, vmem_limit_bytes=V7X_VMEM_LIMIT_BYTES,
                                allow_input_fusion=fuse_inputs)


def _ln_noaffine(x):
    mu = jnp.mean(x, axis=-1, keepdims=True)
    xc = x - mu
    var = jnp.mean(xc * xc, axis=-1, keepdims=True)
    return xc * lax.rsqrt(var + EPS)


def _adaln_kernel(c_ref, w_ref, b_ref, o_ref):
    c = c_ref[...]
    c_act = c * jax.nn.sigmoid(c)
    w = w_ref[...]
    c_hi, w_hi = c_act.astype(BF16), w.astype(BF16)
    c_lo = (c_act - c_hi.astype(F32)).astype(BF16)
    w_lo = (w - w_hi.astype(F32)).astype(BF16)
    o_ref[...] = (jnp.dot(c_hi, w_hi, preferred_element_type=F32) +
                  jnp.dot(c_hi, w_lo, preferred_element_type=F32) +
                  jnp.dot(c_lo, w_hi, preferred_element_type=F32)) + b_ref[...]


def _adaln(c_pad, w, b):
    rows, d = c_pad.shape
    n = w.shape[1]
    cols = 2 * d
    return pl.pallas_call(
        _adaln_kernel,
        out_shape=jax.ShapeDtypeStruct((rows, n), F32),
        grid=(n // cols,),
        in_specs=[pl.BlockSpec((rows, d), lambda j: (0, 0)),
                  pl.BlockSpec((d, cols), lambda j: (0, j)),
                  pl.BlockSpec((1, cols), lambda j: (0, j))],
        out_specs=pl.BlockSpec((rows, cols), lambda j: (0, j)),
        compiler_params=_params("arbitrary"),
        name="adaln",
    )(c_pad, w, b)


def _rope_t(x, tab):
    f = ROPE_FREQS
    cr, sr, cc, sc = tab[0:f], tab[f:2 * f], tab[2 * f:3 * f], tab[3 * f:4 * f]
    x1r, x2r, x1c, x2c = x[0:f], x[f:2 * f], x[2 * f:3 * f], x[3 * f:4 * f]
    return jnp.concatenate([x1r * cr - x2r * sr, x2r * cr + x1r * sr,
                            x1c * cc - x2c * sc, x2c * cc + x1c * sc], axis=0)


def _norm_rope_t(xt, gain, tab):
    ms = jnp.mean(xt * xt, axis=0, keepdims=True)
    return _rope_t(xt * lax.rsqrt(ms + EPS) * gain, tab)


def _inproj_kernel(x_ref, mod_ref, w_ref, gains_ref, tab_ref, qt_ref, k_ref, vt_ref, h_ref, kn_ref,
                   u_ref, proj_ref):
    shift, scale = mod_ref[0:1, :], mod_ref[1:2, :]
    t = x_ref.shape[0]
    half = t // 2
    qkv_w = ATTN_WIDTH + 2 * KV_WIDTH
    conv_ch = (w_ref.shape[1] - qkv_w) // 2

    for r0 in (0, half):
        for r in range(r0, r0 + half, NORM_ROWS):
            u = _ln_noaffine(x_ref[r:r + NORM_ROWS, :]) * (1.0 + scale) + shift
            u_ref[r:r + NORM_ROWS, :] = u.astype(BF16)
        proj_ref[r0:r0 + half, :] = jnp.dot(u_ref[r0:r0 + half, :], w_ref[...],
                                            preferred_element_type=F32)

    for r0 in (0, half):
        cols = slice(r0, r0 + half)
        qkv_t = proj_ref[r0:r0 + half, 0:qkv_w].T
        tab = tab_ref[:, cols]
        gq, gk = gains_ref[0, :, 0:half], gains_ref[1, :, 0:half]
        for h in range(N_HEADS):
            r = h * HEAD_DIM
            qh = _norm_rope_t(qkv_t[r:r + HEAD_DIM], gq, tab) * Q_SCALE
            qt_ref[r:r + HEAD_DIM, cols] = qh.astype(BF16)
        k_rot = []
        for h in range(N_KV_HEADS):
            r = ATTN_WIDTH + h * HEAD_DIM
            kh = _norm_rope_t(qkv_t[r:r + HEAD_DIM], gk, tab).astype(BF16)
            k_rot.append(kh)
            khf = kh.astype(F32)
            kn_ref[h:h + 1, cols] = jnp.sum(khf * khf, axis=0, keepdims=True)
        k_ref[cols, :] = jnp.concatenate(k_rot, axis=0).astype(F32).T.astype(BF16)
        for h in range(N_KV_HEADS):
            r = ATTN_WIDTH + KV_WIDTH + h * HEAD_DIM
            vt_ref[h, 0:HEAD_DIM, cols] = qkv_t[r:r + HEAD_DIM].astype(BF16)
            vt_ref[h, HEAD_DIM:, cols] = jnp.ones((SUM_ROWS, half), BF16)

        a = proj_ref[r0:r0 + half, qkv_w:qkv_w + conv_ch]
        g = proj_ref[r0:r0 + half, qkv_w + conv_ch:]
        h_ref[r0:r0 + half, :] = (a * jax.nn.sigmoid(g)).astype(BF16)


def _inproj(x, mod, w_bf, gains, tab):
    b, s, d = x.shape
    t = ROW_BLOCK
    n_cols = w_bf.shape[1]
    conv_ch = (n_cols - ATTN_WIDTH - 2 * KV_WIDTH) // 2
    return pl.pallas_call(
        _inproj_kernel,
        out_shape=(jax.ShapeDtypeStruct((b, ATTN_WIDTH, s), BF16),
                   jax.ShapeDtypeStruct((b, s, KV_WIDTH), BF16),
                   jax.ShapeDtypeStruct((b, s // t, N_KV_HEADS, V_ROWS, t), BF16),
                   jax.ShapeDtypeStruct((b, s, conv_ch), BF16),
                   jax.ShapeDtypeStruct((b, N_KV_HEADS, s), F32)),
        grid=(b, s // t),
        in_specs=[pl.BlockSpec((None, t, d), lambda bi, i: (bi, i, 0)),
                  pl.BlockSpec((None, 6, d), lambda bi, i: (bi, 0, 0)),
                  pl.BlockSpec((d, n_cols), lambda bi, i: (0, 0)),
                  pl.BlockSpec((2, HEAD_DIM, t), lambda bi, i: (0, 0, 0)),
                  pl.BlockSpec((HEAD_DIM, t), lambda bi, i: (0, i))],
        out_specs=(pl.BlockSpec((None, ATTN_WIDTH, t), lambda bi, i: (bi, 0, i)),
                   pl.BlockSpec((None, t, KV_WIDTH), lambda bi, i: (bi, i, 0)),
                   pl.BlockSpec((None, None, N_KV_HEADS, V_ROWS, t), lambda bi, i: (bi, i, 0, 0, 0)),
                   pl.BlockSpec((None, t, conv_ch), lambda bi, i: (bi, i, 0)),
                   pl.BlockSpec((None, N_KV_HEADS, t), lambda bi, i: (bi, 0, i))),
        scratch_shapes=[pltpu.VMEM((t, d), BF16),
                        pltpu.VMEM((t, n_cols), F32)],
        compiler_params=_params("parallel", "parallel", fuse_inputs=[False, False, True, False, False]),
        name="inproj",
    )(x, mod, w_bf, gains, tab)


def _attn_kernel(qt_ref, k_ref, vt_ref, kn_ref, g_ref, o_ref, qp_ref, s_ref, p_ref, m_ref, acc_ref):
    grp = pl.program_id(1)
    n_chunks, tk = k_ref.shape[0], k_ref.shape[1]
    tq = qt_ref.shape[1]

    first = grp == 0
    k_norm2_max = jnp.max(kn_ref[...], axis=1, keepdims=True)
    bounds = []
    for h in range(GROUPS):
        qh = qt_ref[h * HEAD_DIM:(h + 1) * HEAD_DIM, :]
        z = jnp.zeros_like(qh)
        qp_ref[h, 0:HEAD_DIM, :] = jnp.where(first, qh, z)
        qp_ref[h, HEAD_DIM:2 * HEAD_DIM, :] = jnp.where(first, z, qh)
        qf = qh.astype(F32)
        q_norm2 = jnp.sum(qf * qf, axis=0, keepdims=True)
        bounds.append(jnp.sqrt(q_norm2 * k_norm2_max) * BOUND_SLACK)
    acc_ref[...] = jnp.zeros(acc_ref.shape, F32)
    bound_max = jnp.max(jnp.concatenate(bounds, axis=0))

    def scores(c, h):
        return jnp.dot(k_ref[c], qp_ref[h], preferred_element_type=F32)

    @pl.when(bound_max <= MAX_SAFE_BOUND)
    def _():
        for h in range(GROUPS):
            m_ref[h] = bounds[h]

        def probs(c, h, slot):
            p_ref[slot] = jnp.exp2(scores(c, h) - m_ref[h]).astype(BF16)

        probs(0, 0, 0)

        def chunk(c, carry):
            vc = vt_ref[c]
            c_next = jnp.minimum(c + 1, n_chunks - 1)
            for h in range(GROUPS):
                if h + 1 < GROUPS:
                    probs(c, h + 1, (h + 1) % 2)
                else:
                    probs(c_next, 0, 0)
                acc_ref[h] += jnp.dot(vc, p_ref[h % 2], preferred_element_type=F32)
            return carry

        lax.fori_loop(0, n_chunks, chunk, 0, unroll=CHUNK_UNROLL)

    @pl.when(jnp.logical_not(bound_max <= MAX_SAFE_BOUND))
    def _():
        m_ref[...] = jnp.full(m_ref.shape, -jnp.inf, F32)
        for h in range(SCORES_AHEAD):
            s_ref[h] = scores(0, h)

        def chunk(c, carry):
            vc = vt_ref[c]
            c_next = jnp.minimum(c + 1, n_chunks - 1)
            for h in range(GROUPS):
                ahead = h + SCORES_AHEAD
                if ahead < GROUPS:
                    s_ref[ahead] = scores(c, ahead)
                else:
                    s_ref[ahead - GROUPS] = scores(c_next, ahead - GROUPS)
                s = s_ref[h]
                m_old = m_ref[h]
                slab_max = jnp.max(s.reshape(tk // MAX_SLAB, MAX_SLAB, tq), axis=0)
                m_new = jnp.maximum(m_old, jnp.max(slab_max, axis=0, keepdims=True))
                p = jnp.exp2(s - m_new)
                alpha = jnp.exp2(m_old - m_new)
                pv = jnp.dot(vc, p.astype(BF16), preferred_element_type=F32)
                acc_ref[h] = alpha * acc_ref[h] + pv
                m_ref[h] = m_new
            return carry

        lax.fori_loop(0, n_chunks, chunk, 0, unroll=2)

    outs = []
    for h in range(GROUPS):
        o = acc_ref[h, 0:HEAD_DIM, :] * (1.0 / acc_ref[h, HEAD_DIM:HEAD_DIM + 1, :])
        ms = jnp.mean(o * o, axis=0, keepdims=True)
        outs.append(o * lax.rsqrt(ms + EPS) * g_ref[h * HEAD_DIM:(h + 1) * HEAD_DIM, :])
    o_ref[...] = jnp.concatenate(outs, axis=0).T.astype(BF16)


def _attention(qt, k4, vt4, kn4, gain_t):
    b, _, s = qt.shape
    n_chunks, tk = k4.shape[1], k4.shape[2]
    tq = Q_BLOCK
    gw = GROUPS * HEAD_DIM
    return pl.pallas_call(
        _attn_kernel,
        out_shape=jax.ShapeDtypeStruct((b, s, ATTN_WIDTH), BF16),
        grid=(b, N_KV_HEADS, s // tq),
        in_specs=[pl.BlockSpec((None, gw, tq), lambda bi, g, qi: (bi, g, qi)),
                  pl.BlockSpec((None, n_chunks, tk, KV_WIDTH), lambda bi, g, qi: (bi, 0, 0, 0)),
                  pl.BlockSpec((None, n_chunks, None, V_ROWS, tk), lambda bi, g, qi: (bi, 0, g, 0, 0)),
                  pl.BlockSpec((None, None, 1, s), lambda bi, g, qi: (bi, g, 0, 0)),
                  pl.BlockSpec((None, gw, tq), lambda bi, g, qi: (g, 0, 0))],
        out_specs=pl.BlockSpec((None, tq, gw), lambda bi, g, qi: (bi, qi, g)),
        scratch_shapes=[pltpu.VMEM((GROUPS, 2 * HEAD_DIM, tq), BF16),
                        pltpu.VMEM((GROUPS, tk, tq), F32),
                        pltpu.VMEM((2, tk, tq), BF16),
                        pltpu.VMEM((GROUPS, 1, tq), F32),
                        pltpu.VMEM((GROUPS, V_ROWS, tq), F32)],
        compiler_params=_params("parallel", "parallel", "parallel"),
        name="attn",
    )(qt, k4, vt4, kn4, gain_t)


def _group_mean(sq, gmat):
    hi = sq.astype(BF16)
    lo = (sq - hi.astype(F32)).astype(BF16)
    return (jnp.dot(hi, gmat, preferred_element_type=F32) +
            jnp.dot(lo, gmat, preferred_element_type=F32))


def _convmod_kernel(prev_ref, cur_ref, next_ref, dw_ref, vec_ref, pw_ref, gmat_ref, o_ref, ext_ref, z_ref):
    i = pl.program_id(1)
    last = pl.num_programs(1) - 1
    t = cur_ref.shape[0]
    n_slab = cur_ref.shape[1] // LANES
    prev = jnp.where(i > 0, prev_ref[...].astype(F32), 0.0)
    nxt = jnp.where(i < last, next_ref[...].astype(F32), 0.0)
    for sl in range(n_slab):
        lanes = slice(sl * LANES, (sl + 1) * LANES)
        ext_ref[sl, 0:HALO, :] = prev[:, lanes]
        ext_ref[sl, HALO:HALO + t, :] = cur_ref[:, lanes].astype(F32)
        ext_ref[sl, HALO + t:, :] = nxt[:, lanes]

    dw_b, ln_g, ln_b, out_g = vec_ref[0:1, :], vec_ref[1:2, :], vec_ref[2:3, :], vec_ref[3:4, :]
    base = HALO - CONV_PAD
    rows_par = CONV_ROWS // 2
    for r in range(0, t, CONV_ROWS):
        for parity in range(2):
            slabs = []
            for sl in range(n_slab):
                lanes = slice(sl * LANES, (sl + 1) * LANES)
                acc = jnp.broadcast_to(dw_b[:, lanes], (rows_par, LANES))
                for j in range(CONV_WIDTH):
                    window = ext_ref[sl, pl.ds(r + parity + j + base, rows_par, stride=2), :]
                    acc = acc + dw_ref[j:j + 1, lanes] * window
                slabs.append(acc)
            y = _ln_noaffine(jnp.concatenate(slabs, axis=1)) * ln_g + ln_b
            z = y * jax.nn.sigmoid(y)
            for sl in range(n_slab):
                z_ref[sl, pl.ds(r + parity, rows_par, stride=2), :] = z[:, sl * LANES:(sl + 1) * LANES]

    z_rows = jnp.concatenate([z_ref[sl] for sl in range(n_slab)], axis=1).astype(BF16)
    pw = jnp.dot(z_rows, pw_ref[...], preferred_element_type=F32)
    ms = _group_mean(pw * pw, gmat_ref[...])
    o_ref[...] = (pw * lax.rsqrt(ms + EPS) * out_g).astype(BF16)


def _convmod(h, dw_w, vecs, pw_bf, gmat):
    b, s, ch = h.shape
    t = ROW_BLOCK
    hb = t // HALO
    n_halo = s // HALO
    return pl.pallas_call(
        _convmod_kernel,
        out_shape=jax.ShapeDtypeStruct((b, s, ch), BF16),
        grid=(b, s // t),
        in_specs=[pl.BlockSpec((None, HALO, ch), lambda bi, i: (bi, jnp.maximum(i * hb - 1, 0), 0)),
                  pl.BlockSpec((None, t, ch), lambda bi, i: (bi, i, 0)),
                  pl.BlockSpec((None, HALO, ch), lambda bi, i: (bi, jnp.minimum((i + 1) * hb, n_halo - 1), 0)),
                  pl.BlockSpec(dw_w.shape, lambda bi, i: (0, 0)),
                  pl.BlockSpec(vecs.shape, lambda bi, i: (0, 0)),
                  pl.BlockSpec(pw_bf.shape, lambda bi, i: (0, 0)),
                  pl.BlockSpec(gmat.shape, lambda bi, i: (0, 0))],
        out_specs=pl.BlockSpec((None, t, ch), lambda bi, i: (bi, i, 0)),
        scratch_shapes=[pltpu.VMEM((ch // LANES, t + 2 * HALO, LANES), F32),
                        pltpu.VMEM((ch // LANES, t, LANES), F32)],
        compiler_params=_params("parallel", "parallel",
                                fuse_inputs=[False, False, False, False, False, True, False]),
        name="convmod",
    )(h, h, h, dw_w, vecs, pw_bf, gmat)


def _outproj_kernel(alpha, x_ref, a_ref, c_ref, mod_ref, wa_ref, wc_ref, ln_ref, x1_ref, u2_ref, mix_ref):
    t = x_ref.shape[0]
    half = t // 2
    for r in (0, half):
        mix_ref[r:r + half, :] = (
            jnp.dot(a_ref[r:r + half, :], wa_ref[...], preferred_element_type=F32) +
            jnp.dot(c_ref[r:r + half, :], wc_ref[...], preferred_element_type=F32))
    gate1, shift2, scale2 = mod_ref[2:3, :], mod_ref[3:4, :], mod_ref[4:5, :]
    for r in range(0, t, NORM_ROWS):
        y = alpha * x_ref[r:r + NORM_ROWS, :] + gate1 * mix_ref[r:r + NORM_ROWS, :]
        x1 = _ln_noaffine(y) * ln_ref[0:1, :] + ln_ref[1:2, :]
        x1_ref[r:r + NORM_ROWS, :] = x1
        u2_ref[r:r + NORM_ROWS, :] = (_ln_noaffine(x1) * (1.0 + scale2) + shift2).astype(BF16)


def _outproj(alpha, x, attn, hc, mod, wo_a, wo_c, ln1):
    b, s, d = x.shape
    t = ROW_BLOCK
    aw, cw = attn.shape[2], hc.shape[2]
    return pl.pallas_call(
        functools.partial(_outproj_kernel, alpha),
        out_shape=(jax.ShapeDtypeStruct((b, s, d), F32),
                   jax.ShapeDtypeStruct((b, s, d), BF16)),
        grid=(b, s // t),
        in_specs=[pl.BlockSpec((None, t, d), lambda bi, i: (bi, i, 0)),
                  pl.BlockSpec((None, t, aw), lambda bi, i: (bi, i, 0)),
                  pl.BlockSpec((None, t, cw), lambda bi, i: (bi, i, 0)),
                  pl.BlockSpec((None, 6, d), lambda bi, i: (bi, 0, 0)),
                  pl.BlockSpec((aw, d), lambda bi, i: (0, 0)),
                  pl.BlockSpec((cw, d), lambda bi, i: (0, 0)),
                  pl.BlockSpec((2, d), lambda bi, i: (0, 0))],
        out_specs=(pl.BlockSpec((None, t, d), lambda bi, i: (bi, i, 0)),
                   pl.BlockSpec((None, t, d), lambda bi, i: (bi, i, 0))),
        scratch_shapes=[pltpu.VMEM((t, d), F32)],
        compiler_params=_params("parallel", "parallel",
                                fuse_inputs=[False, False, False, False, True, True, False]),
        name="outproj",
    )(x, attn, hc, mod, wo_a, wo_c, ln1)


def _ffn_kernel(alpha, prev_ref, cur_ref, next_ref, x1_ref, mod_ref, wup_ref, dw_ref, wdn_ref, ln_ref,
                o_ref, ext_ref, hv_ref, hg_ref, act_ref, acc_ref):
    i = pl.program_id(1)
    last = pl.num_programs(1) - 1
    t = cur_ref.shape[0]
    d_ff = wdn_ref.shape[0]
    ext_ref[0:HALO, :] = jnp.where(i > 0, prev_ref[...], jnp.zeros_like(prev_ref[...]))
    ext_ref[HALO:HALO + t, :] = cur_ref[...]
    ext_ref[HALO + t:, :] = jnp.where(i < last, next_ref[...], jnp.zeros_like(next_ref[...]))
    ext = ext_ref[...]

    piece = t // FFN_PIECES
    rows_par = piece // 2
    n_slab = FF_CHUNK // LANES

    def conv3(h_ref, slot, sl, col, first):
        w = dw_ref[:, col + sl * LANES:col + (sl + 1) * LANES]
        taps = [h_ref[slot, sl, pl.ds(first + k, rows_par, stride=2), :] for k in range(FFN_CONV_WIDTH)]
        return w[0:1] * taps[0] + w[1:2] * taps[1] + w[2:3] * taps[2] + w[3:4]

    def up_into(h_ref, c, slot):
        r = jnp.dot(ext, wup_ref[:, c:c + FF_CHUNK], preferred_element_type=F32)
        for sl in range(n_slab):
            h_ref[slot, sl] = r[:, sl * LANES:(sl + 1) * LANES]

    def up_val(c, slot):
        up_into(hv_ref, c, slot)

    def up_gate(c, slot):
        up_into(hg_ref, d_ff + c, slot)

    issue_before = {0: up_val, FFN_PIECES // 2: up_gate}
    up_val(0, 0)
    up_gate(0, 0)
    acc_ref[...] = jnp.zeros(acc_ref.shape, F32)
    for n, c in enumerate(range(0, d_ff, FF_CHUNK)):
        slot = n % 2
        for pi in range(FFN_PIECES):
            if c + FF_CHUNK < d_ff and pi in issue_before:
                issue_before[pi](c + FF_CHUNK, 1 - slot)
            r0 = pi * piece
            for sl in range(n_slab):
                for parity in range(2):
                    first = HALO + r0 + parity - 1
                    val = conv3(hv_ref, slot, sl, c, first)
                    gt = conv3(hg_ref, slot, sl, d_ff + c, first)
                    act = 0.5 * gt * (1.0 + lax.erf(gt * (2.0 ** -0.5))) * val
                    act_ref[sl, pl.ds(r0 + parity, rows_par, stride=2), :] = act
            act_rows = jnp.concatenate([act_ref[sl, r0:r0 + piece, :] for sl in range(n_slab)], axis=1)
            acc_ref[r0:r0 + piece, :] += jnp.dot(act_rows.astype(BF16), wdn_ref[c:c + FF_CHUNK, :],
                                                 preferred_element_type=F32)

    gate2 = mod_ref[5:6, :]
    for r in range(0, t, NORM_ROWS):
        y = alpha * x1_ref[r:r + NORM_ROWS, :] + gate2 * acc_ref[r:r + NORM_ROWS, :]
        o_ref[r:r + NORM_ROWS, :] = _ln_noaffine(y) * ln_ref[0:1, :] + ln_ref[1:2, :]


def _ffn(alpha, u2, x1, mod, wup_bf, dw4, wdn_bf, ln2):
    b, s, d = x1.shape
    t = ROW_BLOCK
    hb = t // HALO
    n_halo = s // HALO
    resident = dict(pipeline_mode=pl.Buffered(1))
    return pl.pallas_call(
        functools.partial(_ffn_kernel, alpha),
        out_shape=jax.ShapeDtypeStruct((b, s, d), F32),
        grid=(b, s // t),
        in_specs=[pl.BlockSpec((None, HALO, d), lambda bi, i: (bi, jnp.maximum(i * hb - 1, 0), 0)),
                  pl.BlockSpec((None, t, d), lambda bi, i: (bi, i, 0)),
                  pl.BlockSpec((None, HALO, d), lambda bi, i: (bi, jnp.minimum((i + 1) * hb, n_halo - 1), 0)),
                  pl.BlockSpec((None, t, d), lambda bi, i: (bi, i, 0)),
                  pl.BlockSpec((None, 6, d), lambda bi, i: (bi, 0, 0)),
                  pl.BlockSpec(wup_bf.shape, lambda bi, i: (0, 0), **resident),
                  pl.BlockSpec(dw4.shape, lambda bi, i: (0, 0)),
                  pl.BlockSpec(wdn_bf.shape, lambda bi, i: (0, 0), **resident),
                  pl.BlockSpec((2, d), lambda bi, i: (0, 0))],
        out_specs=pl.BlockSpec((None, t, d), lambda bi, i: (bi, i, 0)),
        scratch_shapes=[pltpu.VMEM((t + 2 * HALO, d), BF16),
                        pltpu.VMEM((2, FF_CHUNK // LANES, t + 2 * HALO, LANES), F32),
                        pltpu.VMEM((2, FF_CHUNK // LANES, t + 2 * HALO, LANES), F32),
                        pltpu.VMEM((FF_CHUNK // LANES, t, LANES), F32),
                        pltpu.VMEM((t, d), F32)],
        compiler_params=_params("parallel", "parallel"),
        name="ffn",
    )(u2, u2, u2, x1, mod, wup_bf, dw4, wdn_bf, ln2)


def _rope_table_t(seq_len):
    pos = jnp.arange(seq_len, dtype=jnp.int32)
    rows = (pos // GRID_W).astype(F32)
    cols = (pos % GRID_W).astype(F32)
    inv_freq = ROPE_THETA ** (-jnp.arange(ROPE_FREQS, dtype=F32) / ROPE_FREQS)
    ang_r = inv_freq[:, None] * rows[None, :]
    ang_c = inv_freq[:, None] * cols[None, :]
    return jnp.concatenate([jnp.cos(ang_r), jnp.sin(ang_r), jnp.cos(ang_c), jnp.sin(ang_c)], axis=0)


def kernel(x, c, w_ada, b_ada, w_in, q_norm_g, k_norm_g, conv_dw_w, conv_dw_b, conv_ln_g, conv_ln_b,
           w_conv_pw2, attn_out_g, conv_out_g, w_o, ln1_g, ln1_b, w_up, ffn_dw_w, ffn_dw_b, w_down,
           ln2_g, ln2_b):
    b, s, d = x.shape
    depth = w_ada.shape[0]
    alpha = (2.0 * depth) ** 0.25
    conv_ch = w_conv_pw2.shape[1]
    assert s % ROW_BLOCK == 0 and s % Q_BLOCK == 0 and s % GRID_W == 0
    assert w_down.shape[1] % FF_CHUNK == 0 and b <= 8

    tab = _rope_table_t(s)
    c_pad = jnp.zeros((8, d), F32).at[:b].set(c)
    gidx = jnp.arange(conv_ch) // CONV_GROUP_DIM
    gmat = jnp.where(gidx[:, None] == gidx[None, :], 1.0 / CONV_GROUP_DIM, 0.0).astype(BF16)

    for l in range(depth):
        mod = _adaln(c_pad, w_ada[l], b_ada[l][None, :])[:b].reshape(b, 6, d)

        gains = jnp.stack([jnp.broadcast_to(q_norm_g[l][:, None], (HEAD_DIM, ROW_BLOCK)),
                           jnp.broadcast_to(k_norm_g[l][:, None], (HEAD_DIM, ROW_BLOCK))])
        qt, k, vt4, h, kn = _inproj(x, mod, w_in[l].astype(BF16), gains, tab)

        k4 = k.reshape(b, s // KV_CHUNK, KV_CHUNK, KV_WIDTH)
        gain_t = jnp.broadcast_to(attn_out_g[l].reshape(N_KV_HEADS, GROUPS * HEAD_DIM, 1),
                                  (N_KV_HEADS, GROUPS * HEAD_DIM, Q_BLOCK))
        attn = _attention(qt, k4, vt4, kn.reshape(b, N_KV_HEADS, 1, s), gain_t)

        dw_w = jnp.zeros((CONV_WIDTH + 1, conv_ch), F32).at[:CONV_WIDTH].set(conv_dw_w[l])
        vecs = jnp.stack([conv_dw_b[l], conv_ln_g[l], conv_ln_b[l], conv_out_g[l].reshape(-1)])
        hc = _convmod(h, dw_w, vecs, w_conv_pw2[l].astype(BF16), gmat)

        wo_bf = w_o[l].astype(BF16)
        x1, u2 = _outproj(alpha, x, attn, hc, mod, wo_bf[:ATTN_WIDTH], wo_bf[ATTN_WIDTH:],
                          jnp.stack([ln1_g[l], ln1_b[l]]))

        dw4 = jnp.concatenate([ffn_dw_w[l], ffn_dw_b[l][None, :]], axis=0)
        x = _ffn(alpha, u2, x1, mod, w_up[l].astype(BF16), dw4, w_down[l].astype(BF16),
                 jnp.stack([ln2_g[l], ln2_b[l]]))
    return x
```
